```python
import math
import jax, jax.numpy as jnp
from jax import lax
import numpy as np


D_MODEL = 1024
BATCH = 4
SEQ = 4096
DEPTH = 4
DEC_BATCH = 16
DEC_SEQ = 2048
PAST_LEN = 128

N_EVEN = (DEPTH + 1) // 2
N_ODD = DEPTH // 2
HY_WIDTH = D_MODEL // 2
HY_SHORT = 3
HY_EMB = 33
HY_BANDS = (HY_EMB - 1) // 2
HY_FILT = 64
HY_INNER = 2
HY_MAX_DECAY = math.log(1e-2) / 0.3
HY_MIN_DECAY = math.log(1e-2) / 1.5
MLA_HEADS = 8
MLA_NOPE = 64
MLA_ROPE = 32
MLA_V = 64
MLA_Q_RANK = 384
MLA_KV_RANK = 256
ROPE_THETA = 10000.0
Q_BLOCK = 128
EVEN_IN = 3 * HY_WIDTH + MLA_Q_RANK + MLA_KV_RANK + MLA_ROPE
EVEN_MIX = HY_WIDTH + MLA_HEADS * MLA_V
CF_WIDTH = D_MODEL // 2
CF_KERNEL = 31
POOL_WIDTH = D_MODEL // 2
POOL_WINDOWS = (2, 4, 8, 16)
POOL_GROUP = POOL_WIDTH // len(POOL_WINDOWS)
ODD_IN = 2 * CF_WIDTH + POOL_WIDTH
ODD_MIX = CF_WIDTH + POOL_WIDTH
D_FF = 4 * D_MODEL
DN_ALPHA = (2 * DEPTH) ** 0.25
DN_BETA = (8 * DEPTH) ** -0.25
LN_EPS = 1e-5
RMS_EPS = 1e-6

kernel_name = 'hyena_mla_conformer_pool_deepnorm_encoder'


def layer_norm(x, g, b):
    xf = x.astype(jnp.float32)
    mu = jnp.mean(xf, axis=-1, keepdims=True)
    var = jnp.mean(jnp.square(xf - mu), axis=-1, keepdims=True)
    return ((xf - mu) * lax.rsqrt(var + LN_EPS) * g + b).astype(x.dtype)


def rms_norm(x, g):
    xf = x.astype(jnp.float32)
    return (xf * lax.rsqrt(jnp.mean(jnp.square(xf), axis=-1, keepdims=True) + RMS_EPS) * g).astype(x.dtype)


def depthwise_conv(x, w, b):
    k, c = w.shape
    y = lax.conv_general_dilated(x, w.reshape(k, 1, c).astype(x.dtype), window_strides=(1,),
                                 padding=[(k // 2, k // 2)],
                                 dimension_numbers=('NWC', 'WIO', 'NWC'),
                                 feature_group_count=c)
    return y + b


def rope_tables(L):
    inv = 1.0 / (ROPE_THETA ** (jnp.arange(0, MLA_ROPE, 2, dtype=jnp.float32) / MLA_ROPE))
    ang = jnp.arange(L, dtype=jnp.float32)[:, None] * inv[None, :]
    return jnp.cos(ang), jnp.sin(ang)


def apply_rope(x, cos, sin):
    x1, x2 = jnp.split(x, 2, axis=-1)
    cos = cos.astype(x.dtype)
    sin = sin.astype(x.dtype)
    return jnp.concatenate([x1 * cos - x2 * sin, x2 * cos + x1 * sin], axis=-1)


def hyena_filter(L, w1, b1, w_inner, b_inner, freq, w_out):
    f32 = jnp.float32
    t = jnp.linspace(0.0, 1.0, L, dtype=f32)[:, None]
    ang = 2.0 * math.pi * jnp.arange(L, dtype=f32)[:, None] / L
    bands = jnp.linspace(1e-4, HY_BANDS - 1, HY_BANDS, dtype=f32)[None, :]
    feats = jnp.concatenate([t, jnp.cos(bands * ang), -jnp.sin(bands * ang)], axis=-1)
    h = jnp.sin(freq * (feats @ w1 + b1))
    for i in range(HY_INNER):
        h = jnp.sin(freq * (h @ w_inner[i] + b_inner[i]))
    k = (h @ w_out).astype(f32)
    deltas = jnp.abs(jnp.linspace(HY_MIN_DECAY, HY_MAX_DECAY, HY_WIDTH, dtype=f32))
    decay = jnp.exp(-t * deltas)
    h_f = k[:, :HY_WIDTH] * decay
    h_b = k[:, HY_WIDTH:] * decay
    return jnp.concatenate([h_f, jnp.zeros((1, HY_WIDTH), f32), h_b[:0:-1]], axis=0)


def hyena_mix(u, conv_w, conv_b, w1, b1, w_inner, b_inner, freq, w_out, skip):
    L = u.shape[1]
    uc = depthwise_conv(u, conv_w, conv_b)
    x0, x1, v = jnp.split(uc, 3, axis=-1)
    z = (x1 * v).astype(jnp.float32)
    g = hyena_filter(L, w1, b1, w_inner, b_inner, freq, w_out)
    n = 2 * L
    zf = jnp.fft.rfft(z, n=n, axis=1)
    gf = jnp.fft.rfft(g, n=n, axis=0)
    y = jnp.fft.irfft(zf * gf[None], n=n, axis=1)[:, :L]
    y = y + z * skip
    return x0 * y.astype(x0.dtype)


def mla_mix(p, q_norm, w_uq, kv_norm, w_ukv):
    B, L, _ = p.shape
    cq, ckv, k_rope = jnp.split(p, [MLA_Q_RANK, MLA_Q_RANK + MLA_KV_RANK], axis=-1)
    q = (rms_norm(cq, q_norm) @ w_uq).reshape(B, L, MLA_HEADS, MLA_NOPE + MLA_ROPE)
    kv = (rms_norm(ckv, kv_norm) @ w_ukv).reshape(B, L, MLA_HEADS, MLA_NOPE + MLA_V)
    q_nope, q_rope = q[..., :MLA_NOPE], q[..., MLA_NOPE:]
    k_nope, v = kv[..., :MLA_NOPE], kv[..., MLA_NOPE:]
    cos, sin = rope_tables(L)
    q_rope = apply_rope(q_rope, cos[:, None, :], sin[:, None, :])
    k_rope = apply_rope(k_rope, cos, sin)
    scale = (MLA_NOPE + MLA_ROPE) ** -0.5
    nb = L // Q_BLOCK

    def to_blocks(a):
        return jnp.moveaxis(a.reshape(B, nb, Q_BLOCK, *a.shape[2:]), 1, 0)

    def attend(qs):
        qn, qr = qs
        s = (jnp.einsum('bqhd,bkhd->bhqk', qn, k_nope).astype(jnp.float32)
             + jnp.einsum('bqhr,bkr->bhqk', qr, k_rope).astype(jnp.float32))
        pr = jax.nn.softmax(s * scale, axis=-1)
        return jnp.einsum('bhqk,bkhd->bqhd', pr.astype(v.dtype), v)

    o = lax.map(attend, (to_blocks(q_nope), to_blocks(q_rope)))
    return jnp.moveaxis(o, 0, 1).reshape(B, L, MLA_HEADS * MLA_V)


def conformer_mix(u, dw_w, dw_b, ln_g, ln_b):
    a, gate = jnp.split(u, 2, axis=-1)
    h = a * jax.nn.sigmoid(gate)
    h = depthwise_conv(h, dw_w, dw_b)
    h = layer_norm(h, ln_g, ln_b)
    return jax.nn.silu(h)


def pool_mix(u, pool_w, pool_scale):
    B, L, _ = u.shape
    uf = u.astype(jnp.float32)
    cs = jnp.concatenate([jnp.zeros((B, 1, POOL_WIDTH), jnp.float32), jnp.cumsum(uf, axis=1)], axis=1)
    pos = jnp.arange(L)
    outs = []
    for gi, w in enumerate(POOL_WINDOWS):
        lo = w // 2
        hi = w - 1 - lo
        start = jnp.clip(pos - lo, 0, L)
        end = jnp.clip(pos + hi + 1, 0, L)
        sl = slice(gi * POOL_GROUP, (gi + 1) * POOL_GROUP)
        c = cs[..., sl]
        mean = (jnp.take(c, end, axis=1) - jnp.take(c, start, axis=1)) / (end - start).astype(jnp.float32)[None, :, None]
        d = (mean - uf[..., sl]).astype(u.dtype)
        outs.append(d @ pool_w[gi])
    return jnp.concatenate(outs, axis=-1) * pool_scale


def even_mixer(x, w, i):
    p = x @ w['ev_w_in'][i]
    hy = hyena_mix(p[..., :3 * HY_WIDTH], w['hy_conv_w'][i], w['hy_conv_b'][i],
                   w['hy_filt_w1'][i], w['hy_filt_b1'][i], w['hy_filt_w_inner'][i],
                   w['hy_filt_b_inner'][i], w['hy_filt_freq'][i], w['hy_filt_w_out'][i],
                   w['hy_skip'][i])
    at = mla_mix(p[..., 3 * HY_WIDTH:], w['mla_q_norm'][i], w['mla_w_uq'][i],
                 w['mla_kv_norm'][i], w['mla_w_ukv'][i])
    return jnp.concatenate([hy, at], axis=-1) @ w['ev_w_out'][i]


def odd_mixer(x, w, i):
    p = x @ w['od_w_in'][i] + w['od_b_in'][i]
    cf = conformer_mix(p[..., :2 * CF_WIDTH], w['cf_dw_w'][i], w['cf_dw_b'][i],
                       w['cf_ln_g'][i], w['cf_ln_b'][i])
    pl = pool_mix(p[..., 2 * CF_WIDTH:], w['pool_w'][i], w['pool_scale'][i])
    return jnp.concatenate([cf, pl], axis=-1) @ w['od_w_out'][i] + w['od_b_out'][i]


def trunk(x, w):
    for layer in range(DEPTH):
        i = layer // 2
        if layer % 2 == 0:
            mix = even_mixer(x, w, i)
        else:
            mix = odd_mixer(x, w, i)
        x = layer_norm(DN_ALPHA * x + mix, w['ln1_g'][layer], w['ln1_b'][layer])
        h = jnp.square(jax.nn.relu(x @ w['mlp_w1'][layer])) @ w['mlp_w2'][layer]
        x = layer_norm(DN_ALPHA * x + h, w['ln2_g'][layer], w['ln2_b'][layer])
    return x


def setup_inputs(seed: int = 0) -> dict:
    key = jax.random.key(seed)
    ks = iter(jax.random.split(key, 64))

    def nrm(shape, scale):
        return scale * jax.random.normal(next(ks), shape, jnp.float32)

    def gain(shape):
        return 1.0 + nrm(shape, 0.01)

    return {
        'x_prompt': nrm((BATCH, SEQ, D_MODEL), 1.0),
        'x_sample': nrm((DEC_BATCH, DEC_SEQ, D_MODEL), 1.0),
        'ev_w_in': nrm((N_EVEN, D_MODEL, EVEN_IN), D_MODEL ** -0.5),
        'hy_conv_w': nrm((N_EVEN, HY_SHORT, 3 * HY_WIDTH), HY_SHORT ** -0.5),
        'hy_conv_b': nrm((N_EVEN, 3 * HY_WIDTH), 0.01),
        'hy_filt_w1': nrm((N_EVEN, HY_EMB, HY_FILT), HY_EMB ** -0.5),
        'hy_filt_b1': nrm((N_EVEN, HY_FILT), 0.1),
        'hy_filt_w_inner': nrm((N_EVEN, HY_INNER, HY_FILT, HY_FILT), HY_FILT ** -0.5),
        'hy_filt_b_inner': nrm((N_EVEN, HY_INNER, HY_FILT), 0.1),
        'hy_filt_freq': 1.0 + nrm((N_EVEN, HY_FILT), 0.1),
        'hy_filt_w_out': nrm((N_EVEN, HY_FILT, 2 * HY_WIDTH), HY_FILT ** -0.5),
        'hy_skip': nrm((N_EVEN, HY_WIDTH), 1.0),
        'mla_q_norm': gain((N_EVEN, MLA_Q_RANK)),
        'mla_w_uq': nrm((N_EVEN, MLA_Q_RANK, MLA_HEADS * (MLA_NOPE + MLA_ROPE)), MLA_Q_RANK ** -0.5),
        'mla_kv_norm': gain((N_EVEN, MLA_KV_RANK)),
        'mla_w_ukv': nrm((N_EVEN, MLA_KV_RANK, MLA_HEADS * (MLA_NOPE + MLA_V)), MLA_KV_RANK ** -0.5),
        'ev_w_out': nrm((N_EVEN, EVEN_MIX, D_MODEL), EVEN_MIX ** -0.5 * DN_BETA),
        'od_w_in': nrm((N_ODD, D_MODEL, ODD_IN), D_MODEL ** -0.5),
        'od_b_in': nrm((N_ODD, ODD_IN), 0.01),
        'cf_dw_w': nrm((N_ODD, CF_KERNEL, CF_WIDTH), CF_KERNEL ** -0.5),
        'cf_dw_b': nrm((N_ODD, CF_WIDTH), 0.01),
        'cf_ln_g': gain((N_ODD, CF_WIDTH)),
        'cf_ln_b': nrm((N_ODD, CF_WIDTH), 0.01),
        'pool_w': nrm((N_ODD, len(POOL_WINDOWS), POOL_GROUP, POOL_GROUP), POOL_GROUP ** -0.5),
        'pool_scale': 1.0 + nrm((N_ODD, POOL_WIDTH), 0.1),
        'od_w_out': nrm((N_ODD, ODD_MIX, D_MODEL), ODD_MIX ** -0.5 * DN_BETA),
        'od_b_out': nrm((N_ODD, D_MODEL), 0.01),
        'ln1_g': gain((DEPTH, D_MODEL)),
        'ln1_b': nrm((DEPTH, D_MODEL), 0.01),
        'mlp_w1': nrm((DEPTH, D_MODEL, D_FF), D_MODEL ** -0.5),
        'mlp_w2': nrm((DEPTH, D_FF, D_MODEL), D_FF ** -0.5 * DN_BETA),
        'ln2_g': gain((DEPTH, D_MODEL)),
        'ln2_b': nrm((DEPTH, D_MODEL), 0.01),
    }


def reference(x_prompt, x_sample, ev_w_in, hy_conv_w, hy_conv_b, hy_filt_w1, hy_filt_b1,
              hy_filt_w_inner, hy_filt_b_inner, hy_filt_freq, hy_filt_w_out, hy_skip,
              mla_q_norm, mla_w_uq, mla_kv_norm, mla_w_ukv, ev_w_out,
              od_w_in, od_b_in, cf_dw_w, cf_dw_b, cf_ln_g, cf_ln_b, pool_w, pool_scale,
              od_w_out, od_b_out, ln1_g, ln1_b, mlp_w1, mlp_w2, ln2_g, ln2_b):
    w = {
        'ev_w_in': ev_w_in, 'hy_conv_w': hy_conv_w, 'hy_conv_b': hy_conv_b,
        'hy_filt_w1': hy_filt_w1, 'hy_filt_b1': hy_filt_b1,
        'hy_filt_w_inner': hy_filt_w_inner, 'hy_filt_b_inner': hy_filt_b_inner,
        'hy_filt_freq': hy_filt_freq, 'hy_filt_w_out': hy_filt_w_out, 'hy_skip': hy_skip,
        'mla_q_norm': mla_q_norm, 'mla_w_uq': mla_w_uq, 'mla_kv_norm': mla_kv_norm,
        'mla_w_ukv': mla_w_ukv, 'ev_w_out': ev_w_out,
        'od_w_in': od_w_in, 'od_b_in': od_b_in, 'cf_dw_w': cf_dw_w, 'cf_dw_b': cf_dw_b,
        'cf_ln_g': cf_ln_g, 'cf_ln_b': cf_ln_b, 'pool_w': pool_w, 'pool_scale': pool_scale,
        'od_w_out': od_w_out, 'od_b_out': od_b_out,
        'ln1_g': ln1_g, 'ln1_b': ln1_b, 'mlp_w1': mlp_w1, 'mlp_w2': mlp_w2,
        'ln2_g': ln2_g, 'ln2_b': ln2_b,
    }
    y_prompt = trunk(x_prompt, w)
    y_sample = trunk(x_sample, w)
    return (y_prompt, y_sample)
```

```python
import functools
import math

import numpy as np
import jax
import jax.numpy as jnp
from jax import lax
from jax.experimental import pallas as pl
from jax.experimental.pallas import tpu as pltpu

F32 = jnp.float32
BF16 = jnp.bfloat16

D_MODEL = 1024
DEPTH = 4
HY_WIDTH = D_MODEL // 2
HY_EMB = 33
HY_BANDS = (HY_EMB - 1) // 2
HY_FILT = 64
HY_INNER = 2
HY_MAX_DECAY = math.log(1e-2) / 0.3
HY_MIN_DECAY = math.log(1e-2) / 1.5
MLA_HEADS = 8
MLA_NOPE = 64
MLA_ROPE = 32
MLA_V = 64
MLA_Q_RANK = 384
MLA_KV_RANK = 256
ROPE_THETA = 10000.0
CF_WIDTH = D_MODEL // 2
CF_KERNEL = 31
POOL_WIDTH = D_MODEL // 2
POOL_WINDOWS = (2, 4, 8, 16)
POOL_GROUP = POOL_WIDTH // len(POOL_WINDOWS)
D_FF = 4 * D_MODEL
DN_ALPHA = (2 * DEPTH) ** 0.25
LN_EPS = 1e-5
RMS_EPS = 1e-6

LANES = 128
SUBLANES = 8
VMEM_LIMIT = 52 * 1024 * 1024
HEAD_PAD = 128
EVEN_COLS = 3 * HY_WIDTH + 6 * LANES
FFT_N2 = 32
FFT_PITCH = FFT_N2 + SUBLANES
HALO = 16


def _params(*sem):
    return pltpu.CompilerParams(dimension_semantics=sem, vmem_limit_bytes=VMEM_LIMIT)


def _layer_norm(r, g, b):
    mu = jnp.mean(r, axis=-1, keepdims=True)
    d = r - mu
    var = jnp.mean(d * d, axis=-1, keepdims=True)
    return d * lax.rsqrt(var + LN_EPS) * g + b


def _bdot(a, b):
    return jnp.dot(a.astype(BF16), b.astype(BF16), preferred_element_type=F32)


def _linear_body(x_ref, w_ref, b_ref, o_ref):
    o_ref[...] = _bdot(x_ref[...], w_ref[...]) + b_ref[...]


def _linear(x, w, b, tm=512):
    t, k = x.shape
    n = w.shape[1]
    return pl.pallas_call(
        _linear_body,
        grid=(t // tm,),
        in_specs=[pl.BlockSpec((tm, k), lambda i: (i, 0)),
                  pl.BlockSpec((k, n), lambda i: (0, 0)),
                  pl.BlockSpec((1, n), lambda i: (0, 0))],
        out_specs=pl.BlockSpec((tm, n), lambda i: (i, 0)),
        out_shape=jax.ShapeDtypeStruct((t, n), F32),
        compiler_params=_params("parallel"),
        name="linear",
    )(x, w, b)


def _proj_ln_body(m1_ref, m2_ref, x_ref, w1_ref, w2_ref, b_ref, g_ref, beta_ref, o_ref):
    y = (jnp.dot(m1_ref[...], w1_ref[...], preferred_element_type=F32)
         + jnp.dot(m2_ref[...], w2_ref[...], preferred_element_type=F32) + b_ref[...])
    o_ref[...] = _layer_norm(DN_ALPHA * x_ref[...] + y, g_ref[...], beta_ref[...])


def _proj_ln(m1, m2, x, w1, w2, b, g, beta, tm=512):
    t, d = x.shape
    k = m1.shape[1]
    row = lambda i: (i, 0)
    fix = lambda i: (0, 0)
    return pl.pallas_call(
        _proj_ln_body,
        grid=(t // tm,),
        in_specs=[pl.BlockSpec((tm, k), row), pl.BlockSpec((tm, k), row), pl.BlockSpec((tm, d), row),
                  pl.BlockSpec((k, d), fix), pl.BlockSpec((k, d), fix),
                  pl.BlockSpec((1, d), fix), pl.BlockSpec((1, d), fix), pl.BlockSpec((1, d), fix)],
        out_specs=pl.BlockSpec((tm, d), row),
        out_shape=jax.ShapeDtypeStruct((t, d), F32),
        compiler_params=_params("parallel"),
        name="proj_ln",
    )(m1, m2, x, w1, w2, b, g, beta)


def _mlp_ln_body(x_ref, w1_ref, w2_ref, g_ref, beta_ref, o_ref, *, ff_chunk):
    x = x_ref[...]
    xb = x.astype(BF16)
    acc = DN_ALPHA * x
    for c in range(D_FF // ff_chunk):
        h = jnp.dot(xb, w1_ref[:, c * ff_chunk:(c + 1) * ff_chunk], preferred_element_type=F32)
        h = jnp.maximum(h, 0.0)
        acc = acc + jnp.dot((h * h).astype(BF16), w2_ref[c * ff_chunk:(c + 1) * ff_chunk, :],
                            preferred_element_type=F32)
    o_ref[...] = _layer_norm(acc, g_ref[...], beta_ref[...])


def _mlp_ln(x, w1, w2, g, beta, tm=512, ff_chunk=1024):
    t, d = x.shape
    row = lambda i: (i, 0)
    fix = lambda i: (0, 0)
    return pl.pallas_call(
        functools.partial(_mlp_ln_body, ff_chunk=ff_chunk),
        grid=(t // tm,),
        in_specs=[pl.BlockSpec((tm, d), row),
                  pl.BlockSpec((d, D_FF), fix, pipeline_mode=pl.Buffered(1)),
                  pl.BlockSpec((D_FF, d), fix, pipeline_mode=pl.Buffered(1)),
                  pl.BlockSpec((1, d), fix), pl.BlockSpec((1, d), fix)],
        out_specs=pl.BlockSpec((tm, d), row),
        out_shape=jax.ShapeDtypeStruct((t, d), F32),
        compiler_params=_params("parallel"),
        name="mlp_ln",
    )(x, w1, w2, g, beta)


def _halo_specs(tl, width, col_block):
    per = tl // HALO

    def main(b, i):
        return (b, i, col_block)

    def left(b, i):
        return (b, jnp.maximum(i * per - 1, 0), col_block)

    def right(b, i, nblk):
        return (b, jnp.minimum((i + 1) * per, nblk - 1), col_block)

    return main, left, right


def _hy_gate_body(um_ref, ul_ref, ur_ref, w_ref, b_ref, z_ref, x0_ref, ext_ref, *, tl):
    i = pl.program_id(1)
    last = pl.num_programs(1) - 1
    ext_ref[0:HALO, :] = jnp.where(i == 0, 0.0, ul_ref[0])
    ext_ref[HALO:HALO + tl, :] = um_ref[0]
    ext_ref[HALO + tl:, :] = jnp.where(i == last, 0.0, ur_ref[0])
    uc = (w_ref[0:1, :] * ext_ref[HALO - 1:HALO - 1 + tl, :]
          + w_ref[1:2, :] * ext_ref[HALO:HALO + tl, :]
          + w_ref[2:3, :] * ext_ref[HALO + 1:HALO + 1 + tl, :] + b_ref[...])
    x0_ref[0] = uc[:, :HY_WIDTH]
    z_ref[0] = uc[:, HY_WIDTH:2 * HY_WIDTH] * uc[:, 2 * HY_WIDTH:]


def _hy_gate(p, conv_w, conv_b, tl=256):
    bsz, L, _ = p.shape
    w = 3 * HY_WIDTH
    nhalo = L // HALO
    main, left, right = _halo_specs(tl, w, 0)
    out_spec = pl.BlockSpec((1, tl, HY_WIDTH), lambda b, i: (b, i, 0))
    return pl.pallas_call(
        functools.partial(_hy_gate_body, tl=tl),
        grid=(bsz, L // tl),
        in_specs=[pl.BlockSpec((1, tl, w), main),
                  pl.BlockSpec((1, HALO, w), left),
                  pl.BlockSpec((1, HALO, w), functools.partial(right, nblk=nhalo)),
                  pl.BlockSpec((3, w), lambda b, i: (0, 0)),
                  pl.BlockSpec((1, w), lambda b, i: (0, 0))],
        out_specs=[out_spec, out_spec],
        out_shape=[jax.ShapeDtypeStruct((bsz, L, HY_WIDTH), F32)] * 2,
        scratch_shapes=[pltpu.VMEM((tl + 2 * HALO, w), F32)],
        compiler_params=_params("parallel", "parallel"),
        name="hy_gate",
    )(p, p, p, conv_w, conv_b)


def _hy_filter_body(w1_ref, b1_ref, wi_ref, bi_ref, fr_ref, wo_ref, g_ref, *, L, tr):
    hi = lax.Precision.HIGHEST
    m = pl.program_id(0) * tr + lax.broadcasted_iota(jnp.int32, (tr, 1), 0)
    j = jnp.where(m < L, m, 2 * L - m).astype(F32)
    t = j * (1.0 / (L - 1))
    ang = j * (2.0 * math.pi / L)
    band_step = (HY_BANDS - 1 - 1e-4) / (HY_BANDS - 1)
    bands = 1e-4 + band_step * lax.broadcasted_iota(jnp.int32, (1, HY_BANDS), 1).astype(F32)
    arg = ang * bands
    freq = fr_ref[...]
    pre = (t * w1_ref[0:1, :]
           + jnp.dot(jnp.cos(arg), w1_ref[1:1 + HY_BANDS, :], precision=hi, preferred_element_type=F32)
           - jnp.dot(jnp.sin(arg), w1_ref[1 + HY_BANDS:, :], precision=hi, preferred_element_type=F32)
           + b1_ref[...])
    h = jnp.sin(freq * pre)
    for l in range(HY_INNER):
        h = jnp.sin(freq * (jnp.dot(h, wi_ref[l], precision=hi, preferred_element_type=F32) + bi_ref[l]))
    k = jnp.dot(h, wo_ref[...], precision=hi, preferred_element_type=F32)
    ch = lax.broadcasted_iota(jnp.int32, (1, HY_WIDTH), 1).astype(F32)
    deltas = jnp.abs(HY_MIN_DECAY + ch * ((HY_MAX_DECAY - HY_MIN_DECAY) / (HY_WIDTH - 1)))
    decay = jnp.exp(-t * deltas)
    taps = jnp.where(m < L, k[:, :HY_WIDTH], k[:, HY_WIDTH:]) * decay
    g_ref[...] = jnp.where(m == L, 0.0, taps)


def _hy_filter(L, w1, b1, w_inner, b_inner, freq, w_out, tr=512):
    n = 2 * L
    full = lambda *shape: pl.BlockSpec(shape, lambda i: (0,) * len(shape))
    return pl.pallas_call(
        functools.partial(_hy_filter_body, L=L, tr=tr),
        grid=(n // tr,),
        in_specs=[full(HY_EMB, HY_FILT), full(1, HY_FILT), full(HY_INNER, HY_FILT, HY_FILT),
                  full(HY_INNER, 1, HY_FILT), full(1, HY_FILT), full(HY_FILT, 2 * HY_WIDTH)],
        out_specs=pl.BlockSpec((tr, HY_WIDTH), lambda i: (i, 0)),
        out_shape=jax.ShapeDtypeStruct((n, HY_WIDTH), F32),
        compiler_params=_params("parallel"),
        name="hy_filter",
    )(w1, b1.reshape(1, -1), w_inner, b_inner.reshape(HY_INNER, 1, HY_FILT), freq.reshape(1, -1), w_out)


def _fft_dims(L):
    n = 2 * L
    n1 = n // FFT_N2
    return n, n1, n1 // 2 + SUBLANES, n1 // 2


@functools.lru_cache(maxsize=None)
def _fft_tables(L):
    n, n1, kh, h1 = _fft_dims(L)
    n2 = FFT_N2
    k1 = np.arange(kh, dtype=np.float64)[:, None]
    t1 = np.arange(n1, dtype=np.float64)[None, :]
    th = 2.0 * np.pi * k1 * t1 / n1
    fy = np.concatenate([np.cos(th), -np.sin(th)], axis=0)
    k2 = np.arange(n2, dtype=np.float64)[None, :, None]
    t2 = np.arange(n2, dtype=np.float64)[None, None, :]
    ph = 2.0 * np.pi * (k2 * t2 / n2 + k1[:, :, None] * t2 / n)
    mr, mi = np.cos(ph), -np.sin(ph)
    mx = np.concatenate([np.concatenate([mr, -mi], axis=2),
                         np.concatenate([mi, mr], axis=2)], axis=1)
    mxi = np.transpose(mx, (0, 2, 1))
    wk = np.zeros((kh,), np.float64)
    wk[:n1 // 2 + 1] = 2.0
    wk[0] = 1.0
    wk[n1 // 2] = 1.0
    thi = th[:, :h1].T
    fyi = np.concatenate([np.cos(thi) * wk[None, :], -np.sin(thi) * wk[None, :]], axis=1)
    return fy, mx, mxi, fyi


def _pitch_rows(src_ref, dst_ref, nblk, batch_idx=None):
    def body(t1, carry):
        s = pl.multiple_of(t1 * FFT_N2, SUBLANES)
        d = pl.multiple_of(t1 * FFT_PITCH, SUBLANES)
        if batch_idx is None:
            dst_ref[pl.ds(d, FFT_N2), :] = src_ref[pl.ds(s, FFT_N2), :]
        else:
            dst_ref[pl.ds(d, FFT_N2), :] = src_ref[batch_idx, pl.ds(s, FFT_N2), :]
        return carry
    lax.fori_loop(0, nblk, body, 0)


def _fft_stage_y(src_ref, fy_ref, ar_ref, ai_ref, nblk, kh):
    fy = fy_ref[...]
    for t2 in range(FFT_N2):
        x = src_ref[pl.ds(t2, nblk, stride=FFT_PITCH), :].astype(BF16)
        a = jnp.dot(fy, x, preferred_element_type=F32)
        ar_ref[pl.ds(t2, kh, stride=FFT_PITCH), :] = a[:kh]
        ai_ref[pl.ds(t2, kh, stride=FFT_PITCH), :] = a[kh:]


def _hy_spectrum_body(g_ref, fy_ref, mx_ref, gr_ref, gi_ref, gs_ref, ar_ref, ai_ref, *, L):
    n, n1, kh, _ = _fft_dims(L)
    _pitch_rows(g_ref, gs_ref, n1)
    _fft_stage_y(gs_ref, fy_ref, ar_ref, ai_ref, n1, kh)

    def body(k1, carry):
        src = pl.multiple_of(k1 * FFT_PITCH, SUBLANES)
        dst = pl.multiple_of(k1 * FFT_N2, SUBLANES)
        a = jnp.concatenate([ar_ref[pl.ds(src, FFT_N2), :], ai_ref[pl.ds(src, FFT_N2), :]], axis=0)
        z = jnp.dot(mx_ref[k1], a.astype(BF16), preferred_element_type=F32) * (1.0 / n)
        gr_ref[pl.ds(dst, FFT_N2), :] = z[:FFT_N2]
        gi_ref[pl.ds(dst, FFT_N2), :] = z[FFT_N2:]
        return carry
    lax.fori_loop(0, kh, body, 0)


def _hy_spectrum(g, L):
    n, n1, kh, _ = _fft_dims(L)
    fy, mx, _, _ = _fft_tables(L)
    out_spec = pl.BlockSpec((kh * FFT_N2, LANES), lambda c: (0, c))
    return pl.pallas_call(
        functools.partial(_hy_spectrum_body, L=L),
        grid=(HY_WIDTH // LANES,),
        in_specs=[pl.BlockSpec((n, LANES), lambda c: (0, c)),
                  pl.BlockSpec((2 * kh, n1), lambda c: (0, 0)),
                  pl.BlockSpec((kh, 2 * FFT_N2, 2 * FFT_N2), lambda c: (0, 0, 0))],
        out_specs=[out_spec, out_spec],
        out_shape=[jax.ShapeDtypeStruct((kh * FFT_N2, HY_WIDTH), F32)] * 2,
        scratch_shapes=[pltpu.VMEM((n1 * FFT_PITCH, LANES), F32),
                        pltpu.VMEM((kh * FFT_PITCH, LANES), F32),
                        pltpu.VMEM((kh * FFT_PITCH, LANES), F32)],
        compiler_params=_params("parallel"),
        name="hy_spectrum",
    )(g, jnp.asarray(fy, BF16), jnp.asarray(mx, BF16))


def _hy_conv_body(z_ref, x0_ref, gr_ref, gi_ref, fy_ref, mx_ref, mxi_ref, fyi_ref, skip_ref, o_ref,
                  zs_ref, ar_ref, ai_ref, *, L):
    n, n1, kh, h1 = _fft_dims(L)
    _pitch_rows(z_ref, zs_ref, h1, batch_idx=0)
    _fft_stage_y(zs_ref, fy_ref, ar_ref, ai_ref, h1, kh)

    def spectral(k1, carry):
        row = pl.multiple_of(k1 * FFT_PITCH, SUBLANES)
        grow = pl.multiple_of(k1 * FFT_N2, SUBLANES)
        a = jnp.concatenate([ar_ref[pl.ds(row, FFT_N2), :], ai_ref[pl.ds(row, FFT_N2), :]], axis=0)
        z = jnp.dot(mx_ref[k1], a.astype(BF16), preferred_element_type=F32)
        zr, zi = z[:FFT_N2], z[FFT_N2:]
        gr = gr_ref[pl.ds(grow, FFT_N2), :]
        gi = gi_ref[pl.ds(grow, FFT_N2), :]
        y = jnp.concatenate([zr * gr - zi * gi, zr * gi + zi * gr], axis=0)
        b = jnp.dot(mxi_ref[k1], y.astype(BF16), preferred_element_type=F32)
        ar_ref[pl.ds(row, FFT_N2), :] = b[:FFT_N2]
        ai_ref[pl.ds(row, FFT_N2), :] = b[FFT_N2:]
        return carry
    lax.fori_loop(0, kh, spectral, 0)

    fyi = fyi_ref[...]
    for t2 in range(FFT_N2):
        b = jnp.concatenate([ar_ref[pl.ds(t2, kh, stride=FFT_PITCH), :],
                             ai_ref[pl.ds(t2, kh, stride=FFT_PITCH), :]], axis=0)
        zs_ref[pl.ds(t2, h1, stride=FFT_PITCH), :] = jnp.dot(fyi, b.astype(BF16), preferred_element_type=F32)

    skip = skip_ref[...]

    def finish(t1, carry):
        s = pl.multiple_of(t1 * FFT_N2, SUBLANES)
        d = pl.multiple_of(t1 * FFT_PITCH, SUBLANES)
        y = zs_ref[pl.ds(d, FFT_N2), :] + z_ref[0, pl.ds(s, FFT_N2), :] * skip
        o_ref[0, pl.ds(s, FFT_N2), :] = (x0_ref[0, pl.ds(s, FFT_N2), :] * y).astype(o_ref.dtype)
        return carry
    lax.fori_loop(0, h1, finish, 0)


def _hy_conv(z, x0c, gr, gi, skip, L):
    bsz = z.shape[0]
    n, n1, kh, h1 = _fft_dims(L)
    fy, mx, mxi, fyi = _fft_tables(L)
    seq = pl.BlockSpec((1, L, LANES), lambda c, b: (b, 0, c))
    spec = pl.BlockSpec((kh * FFT_N2, LANES), lambda c, b: (0, c))
    mat = pl.BlockSpec((kh, 2 * FFT_N2, 2 * FFT_N2), lambda c, b: (0, 0, 0))
    return pl.pallas_call(
        functools.partial(_hy_conv_body, L=L),
        grid=(HY_WIDTH // LANES, bsz),
        in_specs=[seq, seq, spec, spec,
                  pl.BlockSpec((2 * kh, h1), lambda c, b: (0, 0)),
                  mat, mat,
                  pl.BlockSpec((h1, 2 * kh), lambda c, b: (0, 0)),
                  pl.BlockSpec((1, LANES), lambda c, b: (0, c))],
        out_specs=seq,
        out_shape=jax.ShapeDtypeStruct((bsz, L, HY_WIDTH), BF16),
        scratch_shapes=[pltpu.VMEM((h1 * FFT_PITCH, LANES), F32),
                        pltpu.VMEM((kh * FFT_PITCH, LANES), F32),
                        pltpu.VMEM((kh * FFT_PITCH, LANES), F32)],
        compiler_params=_params("parallel", "parallel"),
        name="hy_conv",
    )(z, x0c, gr, gi, jnp.asarray(fy[:, :h1], BF16), jnp.asarray(mx, BF16), jnp.asarray(mxi, BF16),
      jnp.asarray(fyi, BF16), skip.reshape(1, -1))


def _rms(x, g):
    return x * lax.rsqrt(jnp.mean(x * x, axis=-1, keepdims=True) + RMS_EPS) * g


def _mla_proj_body(p_ref, qn_ref, wqa_ref, wqb_ref, kvn_ref, wk_ref, wv_ref, place_ref,
                   qc_ref, qs_ref, kt_ref, q_ref, k_ref, v_ref):
    p = p_ref[...]
    cq = _rms(p[:, :MLA_Q_RANK], qn_ref[...]).astype(BF16)
    ckv = _rms(p[:, MLA_Q_RANK:MLA_Q_RANK + MLA_KV_RANK], kvn_ref[...]).astype(BF16)
    qa = jnp.dot(cq, wqa_ref[...], preferred_element_type=F32)
    qb = jnp.dot(cq, wqb_ref[...], preferred_element_type=F32)
    qc = qc_ref[...]
    qs = qs_ref[...]
    for h in range(MLA_HEADS):
        sl = slice(h * HEAD_PAD, (h + 1) * HEAD_PAD)
        q_ref[:, sl] = (qa[:, sl] * qc + qb[:, sl] * qs).astype(BF16)
    kr = p[:, MLA_Q_RANK + MLA_KV_RANK:] * kt_ref[...]
    kr = kr + pltpu.roll(kr, LANES - MLA_ROPE, axis=1)
    k = (jnp.dot(ckv, wk_ref[...], preferred_element_type=F32)
         + jnp.dot(kr.astype(BF16), place_ref[...], preferred_element_type=F32))
    k_ref[...] = k.astype(BF16)
    v_ref[...] = jnp.dot(ckv, wv_ref[...], preferred_element_type=F32).astype(BF16)


def _mla_proj(p, L, q_norm, wqa, wqb, kv_norm, wk, wv, place, qc, qs, kt, tm=512):
    t = p.shape[0]
    hw = MLA_HEADS * HEAD_PAD
    nper = L // tm
    fix = lambda i: (0, 0)
    pos = lambda i: (i % nper, 0)
    row = lambda i: (i, 0)
    wide = 6 * LANES
    return pl.pallas_call(
        _mla_proj_body,
        grid=(t // tm,),
        in_specs=[pl.BlockSpec((tm, wide), lambda i: (i, 3 * HY_WIDTH // wide)),
                  pl.BlockSpec((1, MLA_Q_RANK), fix),
                  pl.BlockSpec((MLA_Q_RANK, hw), fix), pl.BlockSpec((MLA_Q_RANK, hw), fix),
                  pl.BlockSpec((1, MLA_KV_RANK), fix),
                  pl.BlockSpec((MLA_KV_RANK, hw), fix),
                  pl.BlockSpec((MLA_KV_RANK, MLA_HEADS * MLA_V), fix),
                  pl.BlockSpec((LANES, hw), fix),
                  pl.BlockSpec((tm, HEAD_PAD), pos), pl.BlockSpec((tm, HEAD_PAD), pos),
                  pl.BlockSpec((tm, LANES), pos)],
        out_specs=[pl.BlockSpec((tm, hw), row), pl.BlockSpec((tm, hw), row),
                   pl.BlockSpec((tm, MLA_HEADS * MLA_V), row)],
        out_shape=[jax.ShapeDtypeStruct((t, hw), BF16), jax.ShapeDtypeStruct((t, hw), BF16),
                   jax.ShapeDtypeStruct((t, MLA_HEADS * MLA_V), BF16)],
        compiler_params=_params("parallel"),
        name="mla_proj",
    )(p, q_norm, wqa, wqb, kv_norm, wk, wv, place, qc, qs, kt)


def _attn_body(q_ref, k_ref, v_ref, o_ref):
    outs = []
    for h in range(2):
        q = q_ref[0, :, h * HEAD_PAD:(h + 1) * HEAD_PAD]
        k = k_ref[0, :, h * HEAD_PAD:(h + 1) * HEAD_PAD]
        v = v_ref[0, :, h * MLA_V:(h + 1) * MLA_V]
        s = lax.dot_general(q, k, (((1,), (1,)), ((), ())), preferred_element_type=F32)
        m = jnp.max(s, axis=-1, keepdims=True)
        e = jnp.exp(s - m)
        l = jnp.sum(e, axis=-1, keepdims=True)
        o = jnp.dot(e.astype(BF16), v, preferred_element_type=F32)
        outs.append(o / l)
    o_ref[0] = jnp.concatenate(outs, axis=-1).astype(o_ref.dtype)


def _attention(q, k, v, tq=256):
    bsz, L, _ = q.shape
    return pl.pallas_call(
        _attn_body,
        grid=(bsz, MLA_HEADS // 2, L // tq),
        in_specs=[pl.BlockSpec((1, tq, 2 * HEAD_PAD), lambda b, g, i: (b, i, g)),
                  pl.BlockSpec((1, L, 2 * HEAD_PAD), lambda b, g, i: (b, 0, g)),
                  pl.BlockSpec((1, L, 2 * MLA_V), lambda b, g, i: (b, 0, g))],
        out_specs=pl.BlockSpec((1, tq, 2 * MLA_V), lambda b, g, i: (b, i, g)),
        out_shape=jax.ShapeDtypeStruct((bsz, L, MLA_HEADS * MLA_V), BF16),
        compiler_params=_params("parallel", "parallel", "parallel"),
        name="attention",
    )(q, k, v)


def _odd_mix_body(um_ref, ul_ref, ur_ref, dw_ref, db_ref, lg_ref, lb_ref, pw_ref, ps_ref,
                  cf_ref, pool_ref, h_ref, u_ref, conv_ref, *, tl, L, rows):
    i = pl.program_id(1)
    last = pl.num_programs(1) - 1

    def glu(u):
        return u[:, :CF_WIDTH] * jax.nn.sigmoid(u[:, CF_WIDTH:2 * CF_WIDTH])

    ul = ul_ref[0]
    um = um_ref[0]
    ur = ur_ref[0]
    h_ref[0:HALO, :] = jnp.where(i == 0, 0.0, glu(ul))
    h_ref[HALO:HALO + tl, :] = glu(um)
    h_ref[HALO + tl:, :] = jnp.where(i == last, 0.0, glu(ur))
    u_ref[0:HALO, :] = jnp.where(i == 0, 0.0, ul[:, 2 * CF_WIDTH:])
    u_ref[HALO:HALO + tl, :] = um[:, 2 * CF_WIDTH:]
    u_ref[HALO + tl:, :] = jnp.where(i == last, 0.0, ur[:, 2 * CF_WIDTH:])

    off = HALO - CF_KERNEL // 2
    for r in range(0, tl, rows):
        for c in range(0, CF_WIDTH, LANES):
            acc = jnp.zeros((rows, LANES), F32) + db_ref[:, c:c + LANES]
            for k in range(CF_KERNEL):
                acc = acc + dw_ref[k:k + 1, c:c + LANES] * h_ref[r + off + k:r + off + k + rows, c:c + LANES]
            conv_ref[r:r + rows, c:c + LANES] = acc
    hn = _layer_norm(conv_ref[...], lg_ref[...], lb_ref[...])
    cf_ref[0] = (hn * jax.nn.sigmoid(hn)).astype(cf_ref.dtype)

    pos = i * tl + lax.broadcasted_iota(jnp.int32, (tl, 1), 0)
    for gi, w in enumerate(POOL_WINDOWS):
        lo = w // 2
        hi = w - 1 - lo
        c0 = gi * POOL_GROUP
        tot = u_ref[HALO - lo:HALO - lo + tl, c0:c0 + POOL_GROUP]
        for d in range(-lo + 1, hi + 1):
            tot = tot + u_ref[HALO + d:HALO + d + tl, c0:c0 + POOL_GROUP]
        cnt = (jnp.minimum(pos + hi + 1, L) - jnp.maximum(pos - lo, 0)).astype(F32)
        dlt = tot / cnt - u_ref[HALO:HALO + tl, c0:c0 + POOL_GROUP]
        pool_ref[0, :, c0:c0 + POOL_GROUP] = (
            jnp.dot(dlt.astype(BF16), pw_ref[gi], preferred_element_type=F32)
            * ps_ref[:, c0:c0 + POOL_GROUP]).astype(pool_ref.dtype)


def _odd_mix(u, dw_w, dw_b, ln_g, ln_b, pool_w, pool_scale, tl=256, rows=64):
    bsz, L, w = u.shape
    main, left, right = _halo_specs(tl, w, 0)
    fix2 = lambda b, i: (0, 0)
    out_spec = pl.BlockSpec((1, tl, CF_WIDTH), lambda b, i: (b, i, 0))
    return pl.pallas_call(
        functools.partial(_odd_mix_body, tl=tl, L=L, rows=rows),
        grid=(bsz, L // tl),
        in_specs=[pl.BlockSpec((1, tl, w), main),
                  pl.BlockSpec((1, HALO, w), left),
                  pl.BlockSpec((1, HALO, w), functools.partial(right, nblk=L // HALO)),
                  pl.BlockSpec((CF_KERNEL, CF_WIDTH), fix2),
                  pl.BlockSpec((1, CF_WIDTH), fix2), pl.BlockSpec((1, CF_WIDTH), fix2),
                  pl.BlockSpec((1, CF_WIDTH), fix2),
                  pl.BlockSpec((len(POOL_WINDOWS), POOL_GROUP, POOL_GROUP), lambda b, i: (0, 0, 0)),
                  pl.BlockSpec((1, POOL_WIDTH), fix2)],
        out_specs=[out_spec, out_spec],
        out_shape=[jax.ShapeDtypeStruct((bsz, L, CF_WIDTH), BF16)] * 2,
        scratch_shapes=[pltpu.VMEM((tl + 2 * HALO, CF_WIDTH), F32),
                        pltpu.VMEM((tl + 2 * HALO, POOL_WIDTH), F32),
                        pltpu.VMEM((tl, CF_WIDTH), F32)],
        compiler_params=_params("parallel", "parallel"),
        name="odd_mix",
    )(u, u, u, dw_w, dw_b, ln_g, ln_b, pool_w, pool_scale)


def _rope_tables(L):
    inv = 1.0 / (ROPE_THETA ** (jnp.arange(0, MLA_ROPE, 2, dtype=F32) / MLA_ROPE))
    ang = jnp.arange(L, dtype=F32)[:, None] * inv[None, :]
    cos, sin = jnp.cos(ang), jnp.sin(ang)
    scale = (MLA_NOPE + MLA_ROPE) ** -0.5
    ones = jnp.ones((L, MLA_NOPE), F32)
    zeros = jnp.zeros((L, MLA_NOPE), F32)
    pad = jnp.zeros((L, HEAD_PAD - MLA_NOPE - MLA_ROPE), F32)
    qc = scale * jnp.concatenate([ones, cos, cos, pad], axis=1)
    qs = scale * jnp.concatenate([zeros, sin, sin, pad], axis=1)
    kt = jnp.concatenate([cos, cos, sin, sin, jnp.zeros((L, LANES - 2 * MLA_ROPE), F32)], axis=1)
    return qc, qs, kt


def _rot_half_cols(w):
    half = w.shape[-1] // 2
    return jnp.concatenate([-w[..., half:], w[..., :half]], axis=-1)


def _prep_even(ev_w_in, mla_w_uq, mla_w_ukv):
    k = ev_w_in.shape[0]
    base = 3 * HY_WIDTH + MLA_Q_RANK + MLA_KV_RANK
    kr = ev_w_in[:, base:base + MLA_ROPE]
    w_in = jnp.concatenate(
        [ev_w_in[:, :base], kr, _rot_half_cols(kr),
         jnp.zeros((k, LANES - 2 * MLA_ROPE), F32)], axis=1).astype(BF16)
    uq = mla_w_uq.reshape(MLA_Q_RANK, MLA_HEADS, MLA_NOPE + MLA_ROPE)
    zq = jnp.zeros((MLA_Q_RANK, MLA_HEADS, HEAD_PAD - MLA_NOPE - MLA_ROPE), F32)
    wqa = jnp.concatenate([uq, zq], axis=2).reshape(MLA_Q_RANK, -1).astype(BF16)
    wqb = jnp.concatenate([jnp.zeros_like(uq[:, :, :MLA_NOPE]), _rot_half_cols(uq[:, :, MLA_NOPE:]), zq],
                          axis=2).reshape(MLA_Q_RANK, -1).astype(BF16)
    ukv = mla_w_ukv.reshape(MLA_KV_RANK, MLA_HEADS, MLA_NOPE + MLA_V)
    zk = jnp.zeros((MLA_KV_RANK, MLA_HEADS, HEAD_PAD - MLA_NOPE), F32)
    wk = jnp.concatenate([ukv[:, :, :MLA_NOPE], zk], axis=2).reshape(MLA_KV_RANK, -1).astype(BF16)
    wv = ukv[:, :, MLA_NOPE:].reshape(MLA_KV_RANK, -1).astype(BF16)
    return w_in, wqa, wqb, wk, wv


def _rope_place():
    e = np.zeros((LANES, MLA_HEADS * HEAD_PAD), np.float32)
    for h in range(MLA_HEADS):
        for j in range(MLA_ROPE):
            e[j, h * HEAD_PAD + MLA_NOPE + j] = 1.0
    return jnp.asarray(e, BF16)


def _trunk(x, w, filt):
    bsz, L, d = x.shape
    t = bsz * L
    xf = x.reshape(t, d)
    qc, qs, kt = _rope_tables(L)
    place = _rope_place()
    zero_b = jnp.zeros((1, d), F32)
    for layer in range(DEPTH):
        i = layer // 2
        if layer % 2 == 0:
            w_in, wqa, wqb, wk, wv = w["even"][i]
            p = _linear(xf, w_in, jnp.zeros((1, EVEN_COLS), F32))
            z, x0c = _hy_gate(p.reshape(bsz, L, EVEN_COLS), w["hy_conv_w"][i], w["hy_conv_b"][i].reshape(1, -1))
            gr, gi = filt[(L, i)]
            m1 = _hy_conv(z, x0c, gr, gi, w["hy_skip"][i], L).reshape(t, HY_WIDTH)
            q, k, v = _mla_proj(p, L, w["mla_q_norm"][i].reshape(1, -1), wqa, wqb,
                                w["mla_kv_norm"][i].reshape(1, -1), wk, wv, place, qc, qs, kt)
            hw = MLA_HEADS * HEAD_PAD
            m2 = _attention(q.reshape(bsz, L, hw), k.reshape(bsz, L, hw),
                            v.reshape(bsz, L, MLA_HEADS * MLA_V)).reshape(t, MLA_HEADS * MLA_V)
            w_out = w["ev_w_out"][i]
            b_out = zero_b
        else:
            u = _linear(xf, w["od_w_in"][i], w["od_b_in"][i].reshape(1, -1))
            cf, pool = _odd_mix(u.reshape(bsz, L, -1), w["cf_dw_w"][i], w["cf_dw_b"][i].reshape(1, -1),
                                w["cf_ln_g"][i].reshape(1, -1), w["cf_ln_b"][i].reshape(1, -1),
                                w["pool_w"][i], w["pool_scale"][i].reshape(1, -1))
            m1 = cf.reshape(t, CF_WIDTH)
            m2 = pool.reshape(t, POOL_WIDTH)
            w_out = w["od_w_out"][i]
            b_out = w["od_b_out"][i].reshape(1, -1)
        half = w_out.shape[0] // 2
        xf = _proj_ln(m1, m2, xf, w_out[:half], w_out[half:], b_out,
                      w["ln1_g"][layer].reshape(1, -1), w["ln1_b"][layer].reshape(1, -1))
        xf = _mlp_ln(xf, w["mlp_w1"][layer], w["mlp_w2"][layer],
                     w["ln2_g"][layer].reshape(1, -1), w["ln2_b"][layer].reshape(1, -1))
    return xf.reshape(bsz, L, d)


def kernel(x_prompt, x_sample, ev_w_in, hy_conv_w, hy_conv_b, hy_filt_w1, hy_filt_b1, hy_filt_w_inner, hy_filt_b_inner, hy_filt_freq, hy_filt_w_out, hy_skip, mla_q_norm, mla_w_uq, mla_kv_norm, mla_w_ukv, ev_w_out, od_w_in, od_b_in, cf_dw_w, cf_dw_b, cf_ln_g, cf_ln_b, pool_w, pool_scale, od_w_out, od_b_out, ln1_g, ln1_b, mlp_w1, mlp_w2, ln2_g, ln2_b):
    n_even = ev_w_in.shape[0]
    w = {
        "even": [_prep_even(ev_w_in[i], mla_w_uq[i], mla_w_ukv[i]) for i in range(n_even)],
        "hy_conv_w": hy_conv_w, "hy_conv_b": hy_conv_b, "hy_skip": hy_skip,
        "mla_q_norm": mla_q_norm, "mla_kv_norm": mla_kv_norm,
        "ev_w_out": ev_w_out.astype(BF16),
        "od_w_in": od_w_in.astype(BF16), "od_b_in": od_b_in,
        "cf_dw_w": cf_dw_w, "cf_dw_b": cf_dw_b, "cf_ln_g": cf_ln_g, "cf_ln_b": cf_ln_b,
        "pool_w": pool_w.astype(BF16), "pool_scale": pool_scale,
        "od_w_out": od_w_out.astype(BF16), "od_b_out": od_b_out,
        "ln1_g": ln1_g, "ln1_b": ln1_b, "ln2_g": ln2_g, "ln2_b": ln2_b,
        "mlp_w1": mlp_w1.astype(BF16), "mlp_w2": mlp_w2.astype(BF16),
    }
    filt = {}
    for L in sorted({x_prompt.shape[1], x_sample.shape[1]}):
        for i in range(n_even):
            g = _hy_filter(L, hy_filt_w1[i], hy_filt_b1[i], hy_filt_w_inner[i], hy_filt_b_inner[i],
                           hy_filt_freq[i], hy_filt_w_out[i])
            filt[(L, i)] = _hy_spectrum(g, L)
    return (_trunk(x_prompt, w, filt), _trunk(x_sample, w, filt))
```

```python
import functools
import math

import numpy as np
import jax
import jax.numpy as jnp
from jax import lax
from jax.experimental import pallas as pl
from jax.experimental.pallas import tpu as pltpu

F32 = jnp.float32
BF16 = jnp.bfloat16

D_MODEL = 1024
DEPTH = 4
HY_WIDTH = D_MODEL // 2
HY_EMB = 33
HY_BANDS = (HY_EMB - 1) // 2
HY_FILT = 64
HY_INNER = 2
HY_MAX_DECAY = math.log(1e-2) / 0.3
HY_MIN_DECAY = math.log(1e-2) / 1.5
MLA_HEADS = 8
MLA_NOPE = 64
MLA_ROPE = 32
MLA_V = 64
MLA_Q_RANK = 384
MLA_KV_RANK = 256
ROPE_THETA = 10000.0
CF_WIDTH = D_MODEL // 2
CF_KERNEL = 31
POOL_WIDTH = D_MODEL // 2
POOL_WINDOWS = (2, 4, 8, 16)
POOL_GROUP = POOL_WIDTH // len(POOL_WINDOWS)
D_FF = 4 * D_MODEL
DN_ALPHA = (2 * DEPTH) ** 0.25
LN_EPS = 1e-5
RMS_EPS = 1e-6

LANES = 128
SUBLANES = 8
VMEM_LIMIT = 52 * 1024 * 1024
HEAD_PAD = 128
EVEN_COLS = 3 * HY_WIDTH + 6 * LANES
FFT_N2 = 32
FFT_PITCH = FFT_N2 + SUBLANES
FFT_UNROLL = 8
HALO = 16


def _params(*sem):
    return pltpu.CompilerParams(dimension_semantics=sem, vmem_limit_bytes=VMEM_LIMIT)


def _layer_norm(r, g, b):
    mu = jnp.mean(r, axis=-1, keepdims=True)
    d = r - mu
    var = jnp.mean(d * d, axis=-1, keepdims=True)
    return d * lax.rsqrt(var + LN_EPS) * g + b


def _bdot(a, b):
    return jnp.dot(a.astype(BF16), b.astype(BF16), preferred_element_type=F32)


def _linear_body(x_ref, w_ref, b_ref, o_ref):
    o_ref[...] = _bdot(x_ref[...], w_ref[...]) + b_ref[...]


def _linear(x, w, b, tm=512):
    t, k = x.shape
    n = w.shape[1]
    return pl.pallas_call(
        _linear_body,
        grid=(t // tm,),
        in_specs=[pl.BlockSpec((tm, k), lambda i: (i, 0)),
                  pl.BlockSpec((k, n), lambda i: (0, 0)),
                  pl.BlockSpec((1, n), lambda i: (0, 0))],
        out_specs=pl.BlockSpec((tm, n), lambda i: (i, 0)),
        out_shape=jax.ShapeDtypeStruct((t, n), F32),
        compiler_params=_params("parallel"),
        name="linear",
    )(x, w, b)


def _proj_ln_body(m1_ref, m2_ref, x_ref, w1_ref, w2_ref, b_ref, g_ref, beta_ref, o_ref):
    y = (jnp.dot(m1_ref[...], w1_ref[...], preferred_element_type=F32)
         + jnp.dot(m2_ref[...], w2_ref[...], preferred_element_type=F32) + b_ref[...])
    o_ref[...] = _layer_norm(DN_ALPHA * x_ref[...] + y, g_ref[...], beta_ref[...])


def _proj_ln(m1, m2, x, w1, w2, b, g, beta, tm=512):
    t, d = x.shape
    k = m1.shape[1]
    row = lambda i: (i, 0)
    fix = lambda i: (0, 0)
    return pl.pallas_call(
        _proj_ln_body,
        grid=(t // tm,),
        in_specs=[pl.BlockSpec((tm, k), row), pl.BlockSpec((tm, k), row), pl.BlockSpec((tm, d), row),
                  pl.BlockSpec((k, d), fix), pl.BlockSpec((k, d), fix),
                  pl.BlockSpec((1, d), fix), pl.BlockSpec((1, d), fix), pl.BlockSpec((1, d), fix)],
        out_specs=pl.BlockSpec((tm, d), row),
        out_shape=jax.ShapeDtypeStruct((t, d), F32),
        compiler_params=_params("parallel"),
        name="proj_ln",
    )(m1, m2, x, w1, w2, b, g, beta)


def _mlp_ln_body(x_ref, w1_ref, w2_ref, g_ref, beta_ref, o_ref, *, ff_chunk):
    x = x_ref[...]
    xb = x.astype(BF16)
    acc = DN_ALPHA * x
    for c in range(D_FF // ff_chunk):
        h = jnp.dot(xb, w1_ref[:, c * ff_chunk:(c + 1) * ff_chunk], preferred_element_type=F32)
        h = jnp.maximum(h, 0.0)
        acc = acc + jnp.dot((h * h).astype(BF16), w2_ref[c * ff_chunk:(c + 1) * ff_chunk, :],
                            preferred_element_type=F32)
    o_ref[...] = _layer_norm(acc, g_ref[...], beta_ref[...])


def _mlp_ln(x, w1, w2, g, beta, tm=512, ff_chunk=1024):
    t, d = x.shape
    row = lambda i: (i, 0)
    fix = lambda i: (0, 0)
    return pl.pallas_call(
        functools.partial(_mlp_ln_body, ff_chunk=ff_chunk),
        grid=(t // tm,),
        in_specs=[pl.BlockSpec((tm, d), row),
                  pl.BlockSpec((d, D_FF), fix, pipeline_mode=pl.Buffered(1)),
                  pl.BlockSpec((D_FF, d), fix, pipeline_mode=pl.Buffered(1)),
                  pl.BlockSpec((1, d), fix), pl.BlockSpec((1, d), fix)],
        out_specs=pl.BlockSpec((tm, d), row),
        out_shape=jax.ShapeDtypeStruct((t, d), F32),
        compiler_params=_params("parallel"),
        name="mlp_ln",
    )(x, w1, w2, g, beta)


def _halo_specs(tl, width, col_block):
    per = tl // HALO

    def main(b, i):
        return (b, i, col_block)

    def left(b, i):
        return (b, jnp.maximum(i * per - 1, 0), col_block)

    def right(b, i, nblk):
        return (b, jnp.minimum((i + 1) * per, nblk - 1), col_block)

    return main, left, right


def _hy_gate_body(um_ref, ul_ref, ur_ref, w_ref, b_ref, z_ref, x0_ref, ext_ref, *, tl):
    i = pl.program_id(1)
    last = pl.num_programs(1) - 1
    ext_ref[0:HALO, :] = jnp.where(i == 0, 0.0, ul_ref[0])
    ext_ref[HALO:HALO + tl, :] = um_ref[0]
    ext_ref[HALO + tl:, :] = jnp.where(i == last, 0.0, ur_ref[0])
    uc = (w_ref[0:1, :] * ext_ref[HALO - 1:HALO - 1 + tl, :]
          + w_ref[1:2, :] * ext_ref[HALO:HALO + tl, :]
          + w_ref[2:3, :] * ext_ref[HALO + 1:HALO + 1 + tl, :] + b_ref[...])
    x0_ref[0] = uc[:, :HY_WIDTH]
    z_ref[0] = uc[:, HY_WIDTH:2 * HY_WIDTH] * uc[:, 2 * HY_WIDTH:]


def _hy_gate(p, conv_w, conv_b, tl=256):
    bsz, L, _ = p.shape
    w = 3 * HY_WIDTH
    nhalo = L // HALO
    main, left, right = _halo_specs(tl, w, 0)
    out_spec = pl.BlockSpec((1, tl, HY_WIDTH), lambda b, i: (b, i, 0))
    return pl.pallas_call(
        functools.partial(_hy_gate_body, tl=tl),
        grid=(bsz, L // tl),
        in_specs=[pl.BlockSpec((1, tl, w), main),
                  pl.BlockSpec((1, HALO, w), left),
                  pl.BlockSpec((1, HALO, w), functools.partial(right, nblk=nhalo)),
                  pl.BlockSpec((3, w), lambda b, i: (0, 0)),
                  pl.BlockSpec((1, w), lambda b, i: (0, 0))],
        out_specs=[out_spec, out_spec],
        out_shape=[jax.ShapeDtypeStruct((bsz, L, HY_WIDTH), F32)] * 2,
        scratch_shapes=[pltpu.VMEM((tl + 2 * HALO, w), F32)],
        compiler_params=_params("parallel", "parallel"),
        name="hy_gate",
    )(p, p, p, conv_w, conv_b)


def _hy_filter_body(w1_ref, b1_ref, wi_ref, bi_ref, fr_ref, wo_ref, g_ref, *, L, tr):
    hi = lax.Precision.HIGHEST
    m = pl.program_id(0) * tr + lax.broadcasted_iota(jnp.int32, (tr, 1), 0)
    j = jnp.where(m < L, m, 2 * L - m).astype(F32)
    t = j * (1.0 / (L - 1))
    ang = j * (2.0 * math.pi / L)
    band_step = (HY_BANDS - 1 - 1e-4) / (HY_BANDS - 1)
    bands = 1e-4 + band_step * lax.broadcasted_iota(jnp.int32, (1, HY_BANDS), 1).astype(F32)
    arg = ang * bands
    freq = fr_ref[...]
    pre = (t * w1_ref[0:1, :]
           + jnp.dot(jnp.cos(arg), w1_ref[1:1 + HY_BANDS, :], precision=hi, preferred_element_type=F32)
           - jnp.dot(jnp.sin(arg), w1_ref[1 + HY_BANDS:, :], precision=hi, preferred_element_type=F32)
           + b1_ref[...])
    h = jnp.sin(freq * pre)
    for l in range(HY_INNER):
        h = jnp.sin(freq * (jnp.dot(h, wi_ref[l], precision=hi, preferred_element_type=F32) + bi_ref[l]))
    k = jnp.dot(h, wo_ref[...], precision=hi, preferred_element_type=F32)
    ch = lax.broadcasted_iota(jnp.int32, (1, HY_WIDTH), 1).astype(F32)
    deltas = jnp.abs(HY_MIN_DECAY + ch * ((HY_MAX_DECAY - HY_MIN_DECAY) / (HY_WIDTH - 1)))
    decay = jnp.exp(-t * deltas)
    taps = jnp.where(m < L, k[:, :HY_WIDTH], k[:, HY_WIDTH:]) * decay
    g_ref[...] = jnp.where(m == L, 0.0, taps)


def _hy_filter(L, w1, b1, w_inner, b_inner, freq, w_out, tr=512):
    n = 2 * L
    full = lambda *shape: pl.BlockSpec(shape, lambda i: (0,) * len(shape))
    return pl.pallas_call(
        functools.partial(_hy_filter_body, L=L, tr=tr),
        grid=(n // tr,),
        in_specs=[full(HY_EMB, HY_FILT), full(1, HY_FILT), full(HY_INNER, HY_FILT, HY_FILT),
                  full(HY_INNER, 1, HY_FILT), full(1, HY_FILT), full(HY_FILT, 2 * HY_WIDTH)],
        out_specs=pl.BlockSpec((tr, HY_WIDTH), lambda i: (i, 0)),
        out_shape=jax.ShapeDtypeStruct((n, HY_WIDTH), F32),
        compiler_params=_params("parallel"),
        name="hy_filter",
    )(w1, b1.reshape(1, -1), w_inner, b_inner.reshape(HY_INNER, 1, HY_FILT), freq.reshape(1, -1), w_out)


def _fft_dims(L):
    n = 2 * L
    n1 = n // FFT_N2
    return n, n1, n1 // 2 + SUBLANES, n1 // 2


@functools.lru_cache(maxsize=None)
def _fft_tables(L):
    n, n1, kh, h1 = _fft_dims(L)
    n2 = FFT_N2
    k1 = np.arange(kh, dtype=np.float64)[:, None]
    t1 = np.arange(n1, dtype=np.float64)[None, :]
    th = 2.0 * np.pi * k1 * t1 / n1
    fy = np.concatenate([np.cos(th), -np.sin(th)], axis=0)
    k2 = np.arange(n2, dtype=np.float64)[None, :, None]
    t2 = np.arange(n2, dtype=np.float64)[None, None, :]
    ph = 2.0 * np.pi * (k2 * t2 / n2 + k1[:, :, None] * t2 / n)
    mr, mi = np.cos(ph), -np.sin(ph)
    mx = np.concatenate([np.concatenate([mr, -mi], axis=2),
                         np.concatenate([mi, mr], axis=2)], axis=1)
    mxi = np.transpose(mx, (0, 2, 1))
    wk = np.zeros((kh,), np.float64)
    wk[:n1 // 2 + 1] = 2.0
    wk[0] = 1.0
    wk[n1 // 2] = 1.0
    thi = th[:, :h1].T
    fyi = np.concatenate([np.cos(thi) * wk[None, :], -np.sin(thi) * wk[None, :]], axis=1)
    return fy, mx, mxi, fyi


def _pitch_rows(src_ref, dst_ref, nblk, batch_idx=None):
    def body(t1, carry):
        s = pl.multiple_of(t1 * FFT_N2, SUBLANES)
        d = pl.multiple_of(t1 * FFT_PITCH, SUBLANES)
        if batch_idx is None:
            dst_ref[pl.ds(d, FFT_N2), :] = src_ref[pl.ds(s, FFT_N2), :]
        else:
            dst_ref[pl.ds(d, FFT_N2), :] = src_ref[batch_idx, pl.ds(s, FFT_N2), :]
        return carry
    lax.fori_loop(0, nblk, body, 0)


def _fft_stage_y(src_ref, fy_ref, ar_ref, ai_ref, nblk, kh):
    fy = fy_ref[...]
    for t2 in range(FFT_N2):
        x = src_ref[pl.ds(t2, nblk, stride=FFT_PITCH), :].astype(BF16)
        a = jnp.dot(fy, x, preferred_element_type=F32)
        ar_ref[pl.ds(t2, kh, stride=FFT_PITCH), :] = a[:kh]
        ai_ref[pl.ds(t2, kh, stride=FFT_PITCH), :] = a[kh:]


def _hy_spectrum_body(g_ref, fy_ref, mx_ref, gr_ref, gi_ref, gs_ref, ar_ref, ai_ref, *, L):
    n, n1, kh, _ = _fft_dims(L)
    _pitch_rows(g_ref, gs_ref, n1)
    _fft_stage_y(gs_ref, fy_ref, ar_ref, ai_ref, n1, kh)

    def body(it, carry):
        k1s = [it * FFT_UNROLL + u for u in range(FFT_UNROLL)]
        a = []
        for k1 in k1s:
            src = pl.multiple_of(k1 * FFT_PITCH, SUBLANES)
            a.append(jnp.concatenate([ar_ref[pl.ds(src, FFT_N2), :], ai_ref[pl.ds(src, FFT_N2), :]],
                                     axis=0).astype(BF16))
        z = [jnp.dot(mx_ref[k1], x, preferred_element_type=F32) * (1.0 / n) for k1, x in zip(k1s, a)]
        for k1, zz in zip(k1s, z):
            dst = pl.multiple_of(k1 * FFT_N2, SUBLANES)
            gr_ref[pl.ds(dst, FFT_N2), :] = zz[:FFT_N2]
            gi_ref[pl.ds(dst, FFT_N2), :] = zz[FFT_N2:]
        return carry
    lax.fori_loop(0, kh // FFT_UNROLL, body, 0)


def _hy_spectrum(g, L):
    n, n1, kh, _ = _fft_dims(L)
    fy, mx, _, _ = _fft_tables(L)
    out_spec = pl.BlockSpec((kh * FFT_N2, LANES), lambda c: (0, c))
    return pl.pallas_call(
        functools.partial(_hy_spectrum_body, L=L),
        grid=(HY_WIDTH // LANES,),
        in_specs=[pl.BlockSpec((n, LANES), lambda c: (0, c)),
                  pl.BlockSpec((2 * kh, n1), lambda c: (0, 0)),
                  pl.BlockSpec((kh, 2 * FFT_N2, 2 * FFT_N2), lambda c: (0, 0, 0))],
        out_specs=[out_spec, out_spec],
        out_shape=[jax.ShapeDtypeStruct((kh * FFT_N2, HY_WIDTH), F32)] * 2,
        scratch_shapes=[pltpu.VMEM((n1 * FFT_PITCH, LANES), F32),
                        pltpu.VMEM((kh * FFT_PITCH, LANES), F32),
                        pltpu.VMEM((kh * FFT_PITCH, LANES), F32)],
        compiler_params=_params("parallel"),
        name="hy_spectrum",
    )(g, jnp.asarray(fy, BF16), jnp.asarray(mx, BF16))


def _hy_conv_body(z_ref, x0_ref, gr_ref, gi_ref, fy_ref, mx_ref, mxi_ref, fyi_ref, skip_ref, o_ref,
                  zs_ref, ar_ref, ai_ref, *, L):
    n, n1, kh, h1 = _fft_dims(L)
    _pitch_rows(z_ref, zs_ref, h1, batch_idx=0)
    _fft_stage_y(zs_ref, fy_ref, ar_ref, ai_ref, h1, kh)

    def spectral(it, carry):
        k1s = [it * FFT_UNROLL + u for u in range(FFT_UNROLL)]
        rows = [pl.multiple_of(k1 * FFT_PITCH, SUBLANES) for k1 in k1s]
        grows = [pl.multiple_of(k1 * FFT_N2, SUBLANES) for k1 in k1s]
        a = [jnp.concatenate([ar_ref[pl.ds(r, FFT_N2), :], ai_ref[pl.ds(r, FFT_N2), :]], axis=0).astype(BF16)
             for r in rows]
        z = [jnp.dot(mx_ref[k1], x, preferred_element_type=F32) for k1, x in zip(k1s, a)]
        y = []
        for zz, g in zip(z, grows):
            zr, zi = zz[:FFT_N2], zz[FFT_N2:]
            gr = gr_ref[pl.ds(g, FFT_N2), :]
            gi = gi_ref[pl.ds(g, FFT_N2), :]
            y.append(jnp.concatenate([zr * gr - zi * gi, zr * gi + zi * gr], axis=0).astype(BF16))
        b = [jnp.dot(mxi_ref[k1], x, preferred_element_type=F32) for k1, x in zip(k1s, y)]
        for r, bb in zip(rows, b):
            ar_ref[pl.ds(r, FFT_N2), :] = bb[:FFT_N2]
            ai_ref[pl.ds(r, FFT_N2), :] = bb[FFT_N2:]
        return carry
    lax.fori_loop(0, kh // FFT_UNROLL, spectral, 0)

    fyi = fyi_ref[...]
    for t2 in range(FFT_N2):
        b = jnp.concatenate([ar_ref[pl.ds(t2, kh, stride=FFT_PITCH), :],
                             ai_ref[pl.ds(t2, kh, stride=FFT_PITCH), :]], axis=0)
        zs_ref[pl.ds(t2, h1, stride=FFT_PITCH), :] = jnp.dot(fyi, b.astype(BF16), preferred_element_type=F32)

    skip = skip_ref[...]

    def finish(t1, carry):
        s = pl.multiple_of(t1 * FFT_N2, SUBLANES)
        d = pl.multiple_of(t1 * FFT_PITCH, SUBLANES)
        y = zs_ref[pl.ds(d, FFT_N2), :] + z_ref[0, pl.ds(s, FFT_N2), :] * skip
        o_ref[0, pl.ds(s, FFT_N2), :] = (x0_ref[0, pl.ds(s, FFT_N2), :] * y).astype(o_ref.dtype)
        return carry
    lax.fori_loop(0, h1, finish, 0)


def _hy_conv(z, x0c, gr, gi, skip, L):
    bsz = z.shape[0]
    n, n1, kh, h1 = _fft_dims(L)
    fy, mx, mxi, fyi = _fft_tables(L)
    seq = pl.BlockSpec((1, L, LANES), lambda c, b: (b, 0, c))
    spec = pl.BlockSpec((kh * FFT_N2, LANES), lambda c, b: (0, c))
    mat = pl.BlockSpec((kh, 2 * FFT_N2, 2 * FFT_N2), lambda c, b: (0, 0, 0))
    return pl.pallas_call(
        functools.partial(_hy_conv_body, L=L),
        grid=(HY_WIDTH // LANES, bsz),
        in_specs=[seq, seq, spec, spec,
                  pl.BlockSpec((2 * kh, h1), lambda c, b: (0, 0)),
                  mat, mat,
                  pl.BlockSpec((h1, 2 * kh), lambda c, b: (0, 0)),
                  pl.BlockSpec((1, LANES), lambda c, b: (0, c))],
        out_specs=seq,
        out_shape=jax.ShapeDtypeStruct((bsz, L, HY_WIDTH), BF16),
        scratch_shapes=[pltpu.VMEM((h1 * FFT_PITCH, LANES), F32),
                        pltpu.VMEM((kh * FFT_PITCH, LANES), F32),
                        pltpu.VMEM((kh * FFT_PITCH, LANES), F32)],
        compiler_params=_params("parallel", "parallel"),
        name="hy_conv",
    )(z, x0c, gr, gi, jnp.asarray(fy[:, :h1], BF16), jnp.asarray(mx, BF16), jnp.asarray(mxi, BF16),
      jnp.asarray(fyi, BF16), skip.reshape(1, -1))


def _rms(x, g):
    return x * lax.rsqrt(jnp.mean(x * x, axis=-1, keepdims=True) + RMS_EPS) * g


def _mla_proj_body(p_ref, qn_ref, wqa_ref, wqb_ref, kvn_ref, wk_ref, wv_ref, place_ref,
                   qc_ref, qs_ref, kt_ref, q_ref, k_ref, v_ref):
    p = p_ref[...]
    cq = _rms(p[:, :MLA_Q_RANK], qn_ref[...]).astype(BF16)
    ckv = _rms(p[:, MLA_Q_RANK:MLA_Q_RANK + MLA_KV_RANK], kvn_ref[...]).astype(BF16)
    qa = jnp.dot(cq, wqa_ref[...], preferred_element_type=F32)
    qb = jnp.dot(cq, wqb_ref[...], preferred_element_type=F32)
    qc = qc_ref[...]
    qs = qs_ref[...]
    for h in range(MLA_HEADS):
        sl = slice(h * HEAD_PAD, (h + 1) * HEAD_PAD)
        q_ref[:, sl] = (qa[:, sl] * qc + qb[:, sl] * qs).astype(BF16)
    kr = p[:, MLA_Q_RANK + MLA_KV_RANK:] * kt_ref[...]
    kr = kr + pltpu.roll(kr, LANES - MLA_ROPE, axis=1)
    k = (jnp.dot(ckv, wk_ref[...], preferred_element_type=F32)
         + jnp.dot(kr.astype(BF16), place_ref[...], preferred_element_type=F32))
    k_ref[...] = k.astype(BF16)
    lane = lax.broadcasted_iota(jnp.int32, (1, MLA_HEADS * HEAD_PAD), 1) % HEAD_PAD
    ones = jnp.where(lane == MLA_V, 1.0, 0.0)
    v_ref[...] = (jnp.dot(ckv, wv_ref[...], preferred_element_type=F32) + ones).astype(BF16)


def _mla_proj(p, L, q_norm, wqa, wqb, kv_norm, wk, wv, place, qc, qs, kt, tm=512):
    t = p.shape[0]
    hw = MLA_HEADS * HEAD_PAD
    nper = L // tm
    fix = lambda i: (0, 0)
    pos = lambda i: (i % nper, 0)
    row = lambda i: (i, 0)
    wide = 6 * LANES
    return pl.pallas_call(
        _mla_proj_body,
        grid=(t // tm,),
        in_specs=[pl.BlockSpec((tm, wide), lambda i: (i, 3 * HY_WIDTH // wide)),
                  pl.BlockSpec((1, MLA_Q_RANK), fix),
                  pl.BlockSpec((MLA_Q_RANK, hw), fix), pl.BlockSpec((MLA_Q_RANK, hw), fix),
                  pl.BlockSpec((1, MLA_KV_RANK), fix),
                  pl.BlockSpec((MLA_KV_RANK, hw), fix),
                  pl.BlockSpec((MLA_KV_RANK, hw), fix),
                  pl.BlockSpec((LANES, hw), fix),
                  pl.BlockSpec((tm, HEAD_PAD), pos), pl.BlockSpec((tm, HEAD_PAD), pos),
                  pl.BlockSpec((tm, LANES), pos)],
        out_specs=[pl.BlockSpec((tm, hw), row), pl.BlockSpec((tm, hw), row),
                   pl.BlockSpec((tm, hw), row)],
        out_shape=[jax.ShapeDtypeStruct((t, hw), BF16)] * 3,
        compiler_params=_params("parallel"),
        name="mla_proj",
    )(p, q_norm, wqa, wqb, kv_norm, wk, wv, place, qc, qs, kt)


def _attn_body(q_ref, k_ref, v_ref, o_ref):
    outs = []
    for h in range(2):
        sl = slice(h * HEAD_PAD, (h + 1) * HEAD_PAD)
        s = lax.dot_general(q_ref[0, :, sl], k_ref[0, :, sl], (((1,), (1,)), ((), ())),
                            preferred_element_type=F32)
        m = jnp.max(s, axis=-1, keepdims=True)
        e = jnp.exp2(s - m)
        o = jnp.dot(e.astype(BF16), v_ref[0, :, sl], preferred_element_type=F32)
        outs.append(o[:, :MLA_V] / o[:, MLA_V:MLA_V + 1])
    o_ref[0] = jnp.concatenate(outs, axis=-1).astype(o_ref.dtype)


def _attention(q, k, v, tq=512):
    bsz, L, _ = q.shape
    return pl.pallas_call(
        _attn_body,
        grid=(bsz, MLA_HEADS // 2, L // tq),
        in_specs=[pl.BlockSpec((1, tq, 2 * HEAD_PAD), lambda b, g, i: (b, i, g)),
                  pl.BlockSpec((1, L, 2 * HEAD_PAD), lambda b, g, i: (b, 0, g)),
                  pl.BlockSpec((1, L, 2 * HEAD_PAD), lambda b, g, i: (b, 0, g))],
        out_specs=pl.BlockSpec((1, tq, 2 * MLA_V), lambda b, g, i: (b, i, g)),
        out_shape=jax.ShapeDtypeStruct((bsz, L, MLA_HEADS * MLA_V), BF16),
        compiler_params=_params("parallel", "parallel", "parallel"),
        name="attention",
    )(q, k, v)


def _odd_mix_body(um_ref, ul_ref, ur_ref, dw_ref, db_ref, lg_ref, lb_ref, pw_ref, ps_ref,
                  cf_ref, pool_ref, h_ref, u_ref, conv_ref, hp_ref, *, tl, L, rows):
    i = pl.program_id(1)
    last = pl.num_programs(1) - 1

    def glu(u):
        return u[:, :CF_WIDTH] * jax.nn.sigmoid(u[:, CF_WIDTH:2 * CF_WIDTH])

    ul = ul_ref[0]
    um = um_ref[0]
    ur = ur_ref[0]
    h_ref[0:HALO, :] = jnp.where(i == 0, 0.0, glu(ul))
    h_ref[HALO:HALO + tl, :] = glu(um)
    h_ref[HALO + tl:, :] = jnp.where(i == last, 0.0, glu(ur))
    u_ref[0:HALO, :] = jnp.where(i == 0, 0.0, ul[:, 2 * CF_WIDTH:])
    u_ref[HALO:HALO + tl, :] = um[:, 2 * CF_WIDTH:]
    u_ref[HALO + tl:, :] = jnp.where(i == last, 0.0, ur[:, 2 * CF_WIDTH:])

    span = tl + 2 * HALO - SUBLANES
    for j in range(SUBLANES):
        hp_ref[j, 0:span, :] = h_ref[j:j + span, :]

    off = HALO - CF_KERNEL // 2
    for r in range(0, tl, rows):
        for c in range(0, CF_WIDTH, LANES):
            acc = jnp.zeros((rows, LANES), F32) + db_ref[:, c:c + LANES]
            for k in range(CF_KERNEL):
                j = (off + k) % SUBLANES
                base = r + off + k - j
                acc = acc + dw_ref[k:k + 1, c:c + LANES] * hp_ref[j, base:base + rows, c:c + LANES]
            conv_ref[r:r + rows, c:c + LANES] = acc
    hn = _layer_norm(conv_ref[...], lg_ref[...], lb_ref[...])
    cf_ref[0] = (hn * jax.nn.sigmoid(hn)).astype(cf_ref.dtype)

    pos = i * tl + lax.broadcasted_iota(jnp.int32, (tl, 1), 0)
    for gi, w in enumerate(POOL_WINDOWS):
        lo = w // 2
        hi = w - 1 - lo
        c0 = gi * POOL_GROUP
        tot = u_ref[HALO - lo:HALO - lo + tl, c0:c0 + POOL_GROUP]
        for d in range(-lo + 1, hi + 1):
            tot = tot + u_ref[HALO + d:HALO + d + tl, c0:c0 + POOL_GROUP]
        cnt = (jnp.minimum(pos + hi + 1, L) - jnp.maximum(pos - lo, 0)).astype(F32)
        dlt = tot / cnt - u_ref[HALO:HALO + tl, c0:c0 + POOL_GROUP]
        pool_ref[0, :, c0:c0 + POOL_GROUP] = (
            jnp.dot(dlt.astype(BF16), pw_ref[gi], preferred_element_type=F32)
            * ps_ref[:, c0:c0 + POOL_GROUP]).astype(pool_ref.dtype)


def _odd_mix(u, dw_w, dw_b, ln_g, ln_b, pool_w, pool_scale, tl=256, rows=64):
    bsz, L, w = u.shape
    main, left, right = _halo_specs(tl, w, 0)
    fix2 = lambda b, i: (0, 0)
    out_spec = pl.BlockSpec((1, tl, CF_WIDTH), lambda b, i: (b, i, 0))
    return pl.pallas_call(
        functools.partial(_odd_mix_body, tl=tl, L=L, rows=rows),
        grid=(bsz, L // tl),
        in_specs=[pl.BlockSpec((1, tl, w), main),
                  pl.BlockSpec((1, HALO, w), left),
                  pl.BlockSpec((1, HALO, w), functools.partial(right, nblk=L // HALO)),
                  pl.BlockSpec((CF_KERNEL, CF_WIDTH), fix2),
                  pl.BlockSpec((1, CF_WIDTH), fix2), pl.BlockSpec((1, CF_WIDTH), fix2),
                  pl.BlockSpec((1, CF_WIDTH), fix2),
                  pl.BlockSpec((len(POOL_WINDOWS), POOL_GROUP, POOL_GROUP), lambda b, i: (0, 0, 0)),
                  pl.BlockSpec((1, POOL_WIDTH), fix2)],
        out_specs=[out_spec, out_spec],
        out_shape=[jax.ShapeDtypeStruct((bsz, L, CF_WIDTH), BF16)] * 2,
        scratch_shapes=[pltpu.VMEM((tl + 2 * HALO, CF_WIDTH), F32),
                        pltpu.VMEM((tl + 2 * HALO, POOL_WIDTH), F32),
                        pltpu.VMEM((tl, CF_WIDTH), F32),
                        pltpu.VMEM((SUBLANES, tl + 2 * HALO, CF_WIDTH), F32)],
        compiler_params=_params("parallel", "parallel"),
        name="odd_mix",
    )(u, u, u, dw_w, dw_b, ln_g, ln_b, pool_w, pool_scale)


def _rope_tables(L):
    inv = 1.0 / (ROPE_THETA ** (jnp.arange(0, MLA_ROPE, 2, dtype=F32) / MLA_ROPE))
    ang = jnp.arange(L, dtype=F32)[:, None] * inv[None, :]
    cos, sin = jnp.cos(ang), jnp.sin(ang)
    scale = (MLA_NOPE + MLA_ROPE) ** -0.5 * math.log2(math.e)
    ones = jnp.ones((L, MLA_NOPE), F32)
    zeros = jnp.zeros((L, MLA_NOPE), F32)
    pad = jnp.zeros((L, HEAD_PAD - MLA_NOPE - MLA_ROPE), F32)
    qc = scale * jnp.concatenate([ones, cos, cos, pad], axis=1)
    qs = scale * jnp.concatenate([zeros, sin, sin, pad], axis=1)
    kt = jnp.concatenate([cos, cos, sin, sin, jnp.zeros((L, LANES - 2 * MLA_ROPE), F32)], axis=1)
    return qc, qs, kt


def _rot_half_cols(w):
    half = w.shape[-1] // 2
    return jnp.concatenate([-w[..., half:], w[..., :half]], axis=-1)


def _prep_even(ev_w_in, mla_w_uq, mla_w_ukv):
    k = ev_w_in.shape[0]
    base = 3 * HY_WIDTH + MLA_Q_RANK + MLA_KV_RANK
    kr = ev_w_in[:, base:base + MLA_ROPE]
    w_in = jnp.concatenate(
        [ev_w_in[:, :base], kr, _rot_half_cols(kr),
         jnp.zeros((k, LANES - 2 * MLA_ROPE), F32)], axis=1).astype(BF16)
    uq = mla_w_uq.reshape(MLA_Q_RANK, MLA_HEADS, MLA_NOPE + MLA_ROPE)
    zq = jnp.zeros((MLA_Q_RANK, MLA_HEADS, HEAD_PAD - MLA_NOPE - MLA_ROPE), F32)
    wqa = jnp.concatenate([uq, zq], axis=2).reshape(MLA_Q_RANK, -1).astype(BF16)
    wqb = jnp.concatenate([jnp.zeros_like(uq[:, :, :MLA_NOPE]), _rot_half_cols(uq[:, :, MLA_NOPE:]), zq],
                          axis=2).reshape(MLA_Q_RANK, -1).astype(BF16)
    ukv = mla_w_ukv.reshape(MLA_KV_RANK, MLA_HEADS, MLA_NOPE + MLA_V)
    zk = jnp.zeros((MLA_KV_RANK, MLA_HEADS, HEAD_PAD - MLA_NOPE), F32)
    wk = jnp.concatenate([ukv[:, :, :MLA_NOPE], zk], axis=2).reshape(MLA_KV_RANK, -1).astype(BF16)
    zv = jnp.zeros((MLA_KV_RANK, MLA_HEADS, HEAD_PAD - MLA_V), F32)
    wv = jnp.concatenate([ukv[:, :, MLA_NOPE:], zv], axis=2).reshape(MLA_KV_RANK, -1).astype(BF16)
    return w_in, wqa, wqb, wk, wv


def _rope_place():
    e = np.zeros((LANES, MLA_HEADS * HEAD_PAD), np.float32)
    for h in range(MLA_HEADS):
        for j in range(MLA_ROPE):
            e[j, h * HEAD_PAD + MLA_NOPE + j] = 1.0
    return jnp.asarray(e, BF16)


def _trunk(x, w, filt):
    bsz, L, d = x.shape
    t = bsz * L
    xf = x.reshape(t, d)
    qc, qs, kt = _rope_tables(L)
    place = _rope_place()
    zero_b = jnp.zeros((1, d), F32)
    for layer in range(DEPTH):
        i = layer // 2
        if layer % 2 == 0:
            w_in, wqa, wqb, wk, wv = w["even"][i]
            p = _linear(xf, w_in, jnp.zeros((1, EVEN_COLS), F32))
            z, x0c = _hy_gate(p.reshape(bsz, L, EVEN_COLS), w["hy_conv_w"][i], w["hy_conv_b"][i].reshape(1, -1))
            gr, gi = filt[(L, i)]
            m1 = _hy_conv(z, x0c, gr, gi, w["hy_skip"][i], L).reshape(t, HY_WIDTH)
            q, k, v = _mla_proj(p, L, w["mla_q_norm"][i].reshape(1, -1), wqa, wqb,
                                w["mla_kv_norm"][i].reshape(1, -1), wk, wv, place, qc, qs, kt)
            hw = MLA_HEADS * HEAD_PAD
            m2 = _attention(q.reshape(bsz, L, hw), k.reshape(bsz, L, hw),
                            v.reshape(bsz, L, hw)).reshape(t, MLA_HEADS * MLA_V)
            w_out = w["ev_w_out"][i]
            b_out = zero_b
        else:
            u = _linear(xf, w["od_w_in"][i], w["od_b_in"][i].reshape(1, -1))
            cf, pool = _odd_mix(u.reshape(bsz, L, -1), w["cf_dw_w"][i], w["cf_dw_b"][i].reshape(1, -1),
                                w["cf_ln_g"][i].reshape(1, -1), w["cf_ln_b"][i].reshape(1, -1),
                                w["pool_w"][i], w["pool_scale"][i].reshape(1, -1))
            m1 = cf.reshape(t, CF_WIDTH)
            m2 = pool.reshape(t, POOL_WIDTH)
            w_out = w["od_w_out"][i]
            b_out = w["od_b_out"][i].reshape(1, -1)
        half = w_out.shape[0] // 2
        xf = _proj_ln(m1, m2, xf, w_out[:half], w_out[half:], b_out,
                      w["ln1_g"][layer].reshape(1, -1), w["ln1_b"][layer].reshape(1, -1))
        xf = _mlp_ln(xf, w["mlp_w1"][layer], w["mlp_w2"][layer],
                     w["ln2_g"][layer].reshape(1, -1), w["ln2_b"][layer].reshape(1, -1))
    return xf.reshape(bsz, L, d)


def kernel(x_prompt, x_sample, ev_w_in, hy_conv_w, hy_conv_b, hy_filt_w1, hy_filt_b1, hy_filt_w_inner, hy_filt_b_inner, hy_filt_freq, hy_filt_w_out, hy_skip, mla_q_norm, mla_w_uq, mla_kv_norm, mla_w_ukv, ev_w_out, od_w_in, od_b_in, cf_dw_w, cf_dw_b, cf_ln_g, cf_ln_b, pool_w, pool_scale, od_w_out, od_b_out, ln1_g, ln1_b, mlp_w1, mlp_w2, ln2_g, ln2_b):
    n_even = ev_w_in.shape[0]
    w = {
        "even": [_prep_even(ev_w_in[i], mla_w_uq[i], mla_w_ukv[i]) for i in range(n_even)],
        "hy_conv_w": hy_conv_w, "hy_conv_b": hy_conv_b, "hy_skip": hy_skip,
        "mla_q_norm": mla_q_norm, "mla_kv_norm": mla_kv_norm,
        "ev_w_out": ev_w_out.astype(BF16),
        "od_w_in": od_w_in.astype(BF16), "od_b_in": od_b_in,
        "cf_dw_w": cf_dw_w, "cf_dw_b": cf_dw_b, "cf_ln_g": cf_ln_g, "cf_ln_b": cf_ln_b,
        "pool_w": pool_w.astype(BF16), "pool_scale": pool_scale,
        "od_w_out": od_w_out.astype(BF16), "od_b_out": od_b_out,
        "ln1_g": ln1_g, "ln1_b": ln1_b, "ln2_g": ln2_g, "ln2_b": ln2_b,
        "mlp_w1": mlp_w1.astype(BF16), "mlp_w2": mlp_w2.astype(BF16),
    }
    filt = {}
    for L in sorted({x_prompt.shape[1], x_sample.shape[1]}):
        for i in range(n_even):
            g = _hy_filter(L, hy_filt_w1[i], hy_filt_b1[i], hy_filt_w_inner[i], hy_filt_b_inner[i],
                           hy_filt_freq[i], hy_filt_w_out[i])
            filt[(L, i)] = _hy_spectrum(g, L)
    return (_trunk(x_prompt, w, filt), _trunk(x_sample, w, filt))
```

```python
import functools
import math

import numpy as np
import jax
import jax.numpy as jnp
from jax import lax
from jax.experimental import pallas as pl
from jax.experimental.pallas import tpu as pltpu

F32 = jnp.float32
BF16 = jnp.bfloat16

D_MODEL = 1024
DEPTH = 4
HY_WIDTH = D_MODEL // 2
HY_EMB = 33
HY_BANDS = (HY_EMB - 1) // 2
HY_FILT = 64
HY_INNER = 2
HY_MAX_DECAY = math.log(1e-2) / 0.3
HY_MIN_DECAY = math.log(1e-2) / 1.5
MLA_HEADS = 8
MLA_NOPE = 64
MLA_ROPE = 32
MLA_V = 64
MLA_Q_RANK = 384
MLA_KV_RANK = 256
ROPE_THETA = 10000.0
CF_WIDTH = D_MODEL // 2
CF_KERNEL = 31
POOL_WIDTH = D_MODEL // 2
POOL_WINDOWS = (2, 4, 8, 16)
POOL_GROUP = POOL_WIDTH // len(POOL_WINDOWS)
D_FF = 4 * D_MODEL
DN_ALPHA = (2 * DEPTH) ** 0.25
LN_EPS = 1e-5
RMS_EPS = 1e-6

LANES = 128
SUBLANES = 8
VMEM_LIMIT = 52 * 1024 * 1024
HEAD_PAD = 128
VT_ROWS = 80
EVEN_COLS = 3 * HY_WIDTH + 6 * LANES
FFT_N2 = 32
FFT_PITCH = FFT_N2 + SUBLANES
FFT_MAX_UNROLL = 18
HALO = 16


def _params(*sem):
    return pltpu.CompilerParams(dimension_semantics=sem, vmem_limit_bytes=VMEM_LIMIT)


def _layer_norm(r, g, b):
    mu = jnp.mean(r, axis=-1, keepdims=True)
    d = r - mu
    var = jnp.mean(d * d, axis=-1, keepdims=True)
    return d * lax.rsqrt(var + LN_EPS) * g + b


def _bdot(a, b):
    return jnp.dot(a.astype(BF16), b.astype(BF16), preferred_element_type=F32)


def _linear_body(x_ref, w_ref, b_ref, o_ref):
    o_ref[...] = _bdot(x_ref[...], w_ref[...]) + b_ref[...]


def _linear(x, w, b, tm=512):
    t, k = x.shape
    n = w.shape[1]
    return pl.pallas_call(
        _linear_body,
        grid=(t // tm,),
        in_specs=[pl.BlockSpec((tm, k), lambda i: (i, 0)),
                  pl.BlockSpec((k, n), lambda i: (0, 0)),
                  pl.BlockSpec((1, n), lambda i: (0, 0))],
        out_specs=pl.BlockSpec((tm, n), lambda i: (i, 0)),
        out_shape=jax.ShapeDtypeStruct((t, n), F32),
        compiler_params=_params("parallel"),
        name="linear",
    )(x, w, b)


def _layer_tail_body(m1_ref, m2_ref, x_ref, wo1_ref, wo2_ref, bo_ref, g1_ref, b1_ref,
                     w1_ref, w2_ref, g2_ref, b2_ref, o_ref, *, ff_chunk):
    y = (jnp.dot(m1_ref[...], wo1_ref[...], preferred_element_type=F32)
         + jnp.dot(m2_ref[...], wo2_ref[...], preferred_element_type=F32) + bo_ref[...])
    x1 = _layer_norm(DN_ALPHA * x_ref[...] + y, g1_ref[...], b1_ref[...])
    xb = x1.astype(BF16)
    acc = DN_ALPHA * x1
    for c in range(D_FF // ff_chunk):
        h = jnp.dot(xb, w1_ref[:, c * ff_chunk:(c + 1) * ff_chunk], preferred_element_type=F32)
        h = jnp.maximum(h, 0.0)
        acc = acc + jnp.dot((h * h).astype(BF16), w2_ref[c * ff_chunk:(c + 1) * ff_chunk, :],
                            preferred_element_type=F32)
    o_ref[...] = _layer_norm(acc, g2_ref[...], b2_ref[...])


def _layer_tail(m1, m2, x, wo1, wo2, bo, g1, b1, w1, w2, g2, b2, tm=512, ff_chunk=1024):
    t, d = x.shape
    k = m1.shape[1]
    row = lambda i: (i, 0)
    fix = lambda i: (0, 0)
    resident = lambda shape: pl.BlockSpec(shape, fix, pipeline_mode=pl.Buffered(1))
    vec = pl.BlockSpec((1, d), fix)
    return pl.pallas_call(
        functools.partial(_layer_tail_body, ff_chunk=ff_chunk),
        grid=(t // tm,),
        in_specs=[pl.BlockSpec((tm, k), row), pl.BlockSpec((tm, k), row), pl.BlockSpec((tm, d), row),
                  resident((k, d)), resident((k, d)), vec, vec, vec,
                  resident((d, D_FF)), resident((D_FF, d)), vec, vec],
        out_specs=pl.BlockSpec((tm, d), row),
        out_shape=jax.ShapeDtypeStruct((t, d), F32),
        compiler_params=_params("parallel"),
        name="layer_tail",
    )(m1, m2, x, wo1, wo2, bo, g1, b1, w1, w2, g2, b2)


def _halo_specs(tl, width, col_block):
    per = tl // HALO

    def main(b, i):
        return (b, i, col_block)

    def left(b, i):
        return (b, jnp.maximum(i * per - 1, 0), col_block)

    def right(b, i, nblk):
        return (b, jnp.minimum((i + 1) * per, nblk - 1), col_block)

    return main, left, right


def _hy_gate_body(um_ref, ul_ref, ur_ref, w_ref, b_ref, z_ref, x0_ref, ext_ref, *, tl):
    i = pl.program_id(1)
    last = pl.num_programs(1) - 1
    ext_ref[0:HALO, :] = jnp.where(i == 0, 0.0, ul_ref[0])
    ext_ref[HALO:HALO + tl, :] = um_ref[0]
    ext_ref[HALO + tl:, :] = jnp.where(i == last, 0.0, ur_ref[0])
    uc = (w_ref[0:1, :] * ext_ref[HALO - 1:HALO - 1 + tl, :]
          + w_ref[1:2, :] * ext_ref[HALO:HALO + tl, :]
          + w_ref[2:3, :] * ext_ref[HALO + 1:HALO + 1 + tl, :] + b_ref[...])
    x0_ref[0] = uc[:, :HY_WIDTH]
    z_ref[0] = uc[:, HY_WIDTH:2 * HY_WIDTH] * uc[:, 2 * HY_WIDTH:]


def _hy_gate(p, conv_w, conv_b, tl=256):
    bsz, L, _ = p.shape
    w = 3 * HY_WIDTH
    nhalo = L // HALO
    main, left, right = _halo_specs(tl, w, 0)
    out_spec = pl.BlockSpec((1, tl, HY_WIDTH), lambda b, i: (b, i, 0))
    return pl.pallas_call(
        functools.partial(_hy_gate_body, tl=tl),
        grid=(bsz, L // tl),
        in_specs=[pl.BlockSpec((1, tl, w), main),
                  pl.BlockSpec((1, HALO, w), left),
                  pl.BlockSpec((1, HALO, w), functools.partial(right, nblk=nhalo)),
                  pl.BlockSpec((3, w), lambda b, i: (0, 0)),
                  pl.BlockSpec((1, w), lambda b, i: (0, 0))],
        out_specs=[out_spec, out_spec],
        out_shape=[jax.ShapeDtypeStruct((bsz, L, HY_WIDTH), F32)] * 2,
        scratch_shapes=[pltpu.VMEM((tl + 2 * HALO, w), F32)],
        compiler_params=_params("parallel", "parallel"),
        name="hy_gate",
    )(p, p, p, conv_w, conv_b)


def _hy_filter_body(w1_ref, b1_ref, wi_ref, bi_ref, fr_ref, wo_ref, g_ref, *, L, tr):
    hi = lax.Precision.HIGHEST
    m = pl.program_id(0) * tr + lax.broadcasted_iota(jnp.int32, (tr, 1), 0)
    j = jnp.where(m < L, m, 2 * L - m).astype(F32)
    t = j * (1.0 / (L - 1))
    ang = j * (2.0 * math.pi / L)
    band_step = (HY_BANDS - 1 - 1e-4) / (HY_BANDS - 1)
    bands = 1e-4 + band_step * lax.broadcasted_iota(jnp.int32, (1, HY_BANDS), 1).astype(F32)
    arg = ang * bands
    freq = fr_ref[...]
    pre = (t * w1_ref[0:1, :]
           + jnp.dot(jnp.cos(arg), w1_ref[1:1 + HY_BANDS, :], precision=hi, preferred_element_type=F32)
           - jnp.dot(jnp.sin(arg), w1_ref[1 + HY_BANDS:, :], precision=hi, preferred_element_type=F32)
           + b1_ref[...])
    h = jnp.sin(freq * pre)
    for l in range(HY_INNER):
        h = jnp.sin(freq * (jnp.dot(h, wi_ref[l], precision=hi, preferred_element_type=F32) + bi_ref[l]))
    k = jnp.dot(h, wo_ref[...], precision=hi, preferred_element_type=F32)
    ch = lax.broadcasted_iota(jnp.int32, (1, HY_WIDTH), 1).astype(F32)
    deltas = jnp.abs(HY_MIN_DECAY + ch * ((HY_MAX_DECAY - HY_MIN_DECAY) / (HY_WIDTH - 1)))
    decay = jnp.exp(-t * deltas)
    taps = jnp.where(m < L, k[:, :HY_WIDTH], k[:, HY_WIDTH:]) * decay
    g_ref[...] = jnp.where(m == L, 0.0, taps)


def _hy_filter(L, w1, b1, w_inner, b_inner, freq, w_out, tr=512):
    n = 2 * L
    full = lambda *shape: pl.BlockSpec(shape, lambda i: (0,) * len(shape))
    return pl.pallas_call(
        functools.partial(_hy_filter_body, L=L, tr=tr),
        grid=(n // tr,),
        in_specs=[full(HY_EMB, HY_FILT), full(1, HY_FILT), full(HY_INNER, HY_FILT, HY_FILT),
                  full(HY_INNER, 1, HY_FILT), full(1, HY_FILT), full(HY_FILT, 2 * HY_WIDTH)],
        out_specs=pl.BlockSpec((tr, HY_WIDTH), lambda i: (i, 0)),
        out_shape=jax.ShapeDtypeStruct((n, HY_WIDTH), F32),
        compiler_params=_params("parallel"),
        name="hy_filter",
    )(w1, b1.reshape(1, -1), w_inner, b_inner.reshape(HY_INNER, 1, HY_FILT), freq.reshape(1, -1), w_out)


def _fft_dims(L):
    n = 2 * L
    n1 = n // FFT_N2
    return n, n1, n1 // 2 + SUBLANES, n1 // 2


@functools.lru_cache(maxsize=None)
def _fft_tables(L):
    n, n1, kh, h1 = _fft_dims(L)
    n2 = FFT_N2
    k1 = np.arange(kh, dtype=np.float64)[:, None]
    t1 = np.arange(n1, dtype=np.float64)[None, :]
    th = 2.0 * np.pi * k1 * t1 / n1
    fy = np.concatenate([np.cos(th), -np.sin(th)], axis=0)
    k2 = np.arange(n2, dtype=np.float64)[None, :, None]
    t2 = np.arange(n2, dtype=np.float64)[None, None, :]
    ph = 2.0 * np.pi * (k2 * t2 / n2 + k1[:, :, None] * t2 / n)
    mr, mi = np.cos(ph), -np.sin(ph)
    mx = np.concatenate([np.concatenate([mr, -mi], axis=2),
                         np.concatenate([mi, mr], axis=2)], axis=1)
    mxi = np.transpose(mx, (0, 2, 1))
    wk = np.zeros((kh,), np.float64)
    wk[:n1 // 2 + 1] = 2.0
    wk[0] = 1.0
    wk[n1 // 2] = 1.0
    thi = th[:, :h1].T
    fyi = np.concatenate([np.cos(thi) * wk[None, :], -np.sin(thi) * wk[None, :]], axis=1)
    return fy, mx, mxi, fyi


def _fft_unroll(kh):
    return max(u for u in range(1, FFT_MAX_UNROLL + 1) if kh % u == 0)


def _pitch_rows(src_ref, dst_ref, nblk, batch_idx=None):
    def body(t1, carry):
        s = pl.multiple_of(t1 * FFT_N2, SUBLANES)
        d = pl.multiple_of(t1 * FFT_PITCH, SUBLANES)
        if batch_idx is None:
            dst_ref[pl.ds(d, FFT_N2), :] = src_ref[pl.ds(s, FFT_N2), :]
        else:
            dst_ref[pl.ds(d, FFT_N2), :] = src_ref[batch_idx, pl.ds(s, FFT_N2), :]
        return carry
    lax.fori_loop(0, nblk, body, 0)


def _fft_stage_y(src_ref, fy_ref, ar_ref, ai_ref, nblk, kh):
    fy = fy_ref[...]
    for t2 in range(FFT_N2):
        x = src_ref[pl.ds(t2, nblk, stride=FFT_PITCH), :].astype(BF16)
        a = jnp.dot(fy, x, preferred_element_type=F32)
        ar_ref[pl.ds(t2, kh, stride=FFT_PITCH), :] = a[:kh]
        ai_ref[pl.ds(t2, kh, stride=FFT_PITCH), :] = a[kh:]


def _hy_spectrum_body(g_ref, fy_ref, mx_ref, gr_ref, gi_ref, gs_ref, ar_ref, ai_ref, *, L):
    n, n1, kh, _ = _fft_dims(L)
    _pitch_rows(g_ref, gs_ref, n1)
    _fft_stage_y(gs_ref, fy_ref, ar_ref, ai_ref, n1, kh)
    unroll = _fft_unroll(kh)

    def body(it, carry):
        k1s = [it * unroll + u for u in range(unroll)]
        a = []
        for k1 in k1s:
            src = pl.multiple_of(k1 * FFT_PITCH, SUBLANES)
            a.append(jnp.concatenate([ar_ref[pl.ds(src, FFT_N2), :], ai_ref[pl.ds(src, FFT_N2), :]],
                                     axis=0).astype(BF16))
        z = [jnp.dot(mx_ref[k1], x, preferred_element_type=F32) * (1.0 / n) for k1, x in zip(k1s, a)]
        for k1, zz in zip(k1s, z):
            dst = pl.multiple_of(k1 * FFT_N2, SUBLANES)
            gr_ref[pl.ds(dst, FFT_N2), :] = zz[:FFT_N2]
            gi_ref[pl.ds(dst, FFT_N2), :] = zz[FFT_N2:]
        return carry
    lax.fori_loop(0, kh // unroll, body, 0)


def _hy_spectrum(g, L):
    n, n1, kh, _ = _fft_dims(L)
    fy, mx, _, _ = _fft_tables(L)
    out_spec = pl.BlockSpec((kh * FFT_N2, LANES), lambda c: (0, c))
    return pl.pallas_call(
        functools.partial(_hy_spectrum_body, L=L),
        grid=(HY_WIDTH // LANES,),
        in_specs=[pl.BlockSpec((n, LANES), lambda c: (0, c)),
                  pl.BlockSpec((2 * kh, n1), lambda c: (0, 0)),
                  pl.BlockSpec((kh, 2 * FFT_N2, 2 * FFT_N2), lambda c: (0, 0, 0))],
        out_specs=[out_spec, out_spec],
        out_shape=[jax.ShapeDtypeStruct((kh * FFT_N2, HY_WIDTH), F32)] * 2,
        scratch_shapes=[pltpu.VMEM((n1 * FFT_PITCH, LANES), F32),
                        pltpu.VMEM((kh * FFT_PITCH, LANES), F32),
                        pltpu.VMEM((kh * FFT_PITCH, LANES), F32)],
        compiler_params=_params("parallel"),
        name="hy_spectrum",
    )(g, jnp.asarray(fy, BF16), jnp.asarray(mx, BF16))


def _hy_conv_body(z_ref, x0_ref, gr_ref, gi_ref, fy_ref, mx_ref, mxi_ref, fyi_ref, skip_ref, o_ref,
                  zs_ref, ar_ref, ai_ref, *, L):
    n, n1, kh, h1 = _fft_dims(L)
    _pitch_rows(z_ref, zs_ref, h1, batch_idx=0)
    _fft_stage_y(zs_ref, fy_ref, ar_ref, ai_ref, h1, kh)

    unroll = _fft_unroll(kh)

    def spectral(it, carry):
        k1s = [it * unroll + u for u in range(unroll)]
        rows = [pl.multiple_of(k1 * FFT_PITCH, SUBLANES) for k1 in k1s]
        grows = [pl.multiple_of(k1 * FFT_N2, SUBLANES) for k1 in k1s]
        a = [jnp.concatenate([ar_ref[pl.ds(r, FFT_N2), :], ai_ref[pl.ds(r, FFT_N2), :]], axis=0).astype(BF16)
             for r in rows]
        z = [jnp.dot(mx_ref[k1], x, preferred_element_type=F32) for k1, x in zip(k1s, a)]
        y = []
        for zz, g in zip(z, grows):
            zr, zi = zz[:FFT_N2], zz[FFT_N2:]
            gr = gr_ref[pl.ds(g, FFT_N2), :]
            gi = gi_ref[pl.ds(g, FFT_N2), :]
            y.append(jnp.concatenate([zr * gr - zi * gi, zr * gi + zi * gr], axis=0).astype(BF16))
        b = [jnp.dot(mxi_ref[k1], x, preferred_element_type=F32) for k1, x in zip(k1s, y)]
        for r, bb in zip(rows, b):
            ar_ref[pl.ds(r, FFT_N2), :] = bb[:FFT_N2]
            ai_ref[pl.ds(r, FFT_N2), :] = bb[FFT_N2:]
        return carry
    lax.fori_loop(0, kh // unroll, spectral, 0)

    fyi = fyi_ref[...]
    for t2 in range(FFT_N2):
        b = jnp.concatenate([ar_ref[pl.ds(t2, kh, stride=FFT_PITCH), :],
                             ai_ref[pl.ds(t2, kh, stride=FFT_PITCH), :]], axis=0)
        zs_ref[pl.ds(t2, h1, stride=FFT_PITCH), :] = jnp.dot(fyi, b.astype(BF16), preferred_element_type=F32)

    skip = skip_ref[...]

    def finish(t1, carry):
        s = pl.multiple_of(t1 * FFT_N2, SUBLANES)
        d = pl.multiple_of(t1 * FFT_PITCH, SUBLANES)
        y = zs_ref[pl.ds(d, FFT_N2), :] + z_ref[0, pl.ds(s, FFT_N2), :] * skip
        o_ref[0, pl.ds(s, FFT_N2), :] = (x0_ref[0, pl.ds(s, FFT_N2), :] * y).astype(o_ref.dtype)
        return carry
    lax.fori_loop(0, h1, finish, 0)


def _hy_conv(z, x0c, gr, gi, skip, L):
    bsz = z.shape[0]
    n, n1, kh, h1 = _fft_dims(L)
    fy, mx, mxi, fyi = _fft_tables(L)
    seq = pl.BlockSpec((1, L, LANES), lambda c, b: (b, 0, c))
    spec = pl.BlockSpec((kh * FFT_N2, LANES), lambda c, b: (0, c))
    mat = pl.BlockSpec((kh, 2 * FFT_N2, 2 * FFT_N2), lambda c, b: (0, 0, 0))
    return pl.pallas_call(
        functools.partial(_hy_conv_body, L=L),
        grid=(HY_WIDTH // LANES, bsz),
        in_specs=[seq, seq, spec, spec,
                  pl.BlockSpec((2 * kh, h1), lambda c, b: (0, 0)),
                  mat, mat,
                  pl.BlockSpec((h1, 2 * kh), lambda c, b: (0, 0)),
                  pl.BlockSpec((1, LANES), lambda c, b: (0, c))],
        out_specs=seq,
        out_shape=jax.ShapeDtypeStruct((bsz, L, HY_WIDTH), BF16),
        scratch_shapes=[pltpu.VMEM((h1 * FFT_PITCH, LANES), F32),
                        pltpu.VMEM((kh * FFT_PITCH, LANES), F32),
                        pltpu.VMEM((kh * FFT_PITCH, LANES), F32)],
        compiler_params=_params("parallel", "parallel"),
        name="hy_conv",
    )(z, x0c, gr, gi, jnp.asarray(fy[:, :h1], BF16), jnp.asarray(mx, BF16), jnp.asarray(mxi, BF16),
      jnp.asarray(fyi, BF16), skip.reshape(1, -1))


def _rms(x, g):
    return x * lax.rsqrt(jnp.mean(x * x, axis=-1, keepdims=True) + RMS_EPS) * g


def _mla_proj_body(p_ref, qn_ref, wqa_ref, wqb_ref, kvn_ref, wk_ref, wvt_ref, place_ref,
                   qc_ref, qs_ref, kt_ref, q_ref, k_ref, vt_ref):
    p = p_ref[...]
    cq = _rms(p[:, :MLA_Q_RANK], qn_ref[...]).astype(BF16)
    ckv = _rms(p[:, MLA_Q_RANK:MLA_Q_RANK + MLA_KV_RANK], kvn_ref[...]).astype(BF16)
    qa = jnp.dot(cq, wqa_ref[...], preferred_element_type=F32)
    qb = jnp.dot(cq, wqb_ref[...], preferred_element_type=F32)
    qc = qc_ref[...]
    qs = qs_ref[...]
    for h in range(MLA_HEADS):
        sl = slice(h * HEAD_PAD, (h + 1) * HEAD_PAD)
        q_ref[:, sl] = (qa[:, sl] * qc + qb[:, sl] * qs).astype(BF16)
    kr = p[:, MLA_Q_RANK + MLA_KV_RANK:] * kt_ref[...]
    kr = kr + pltpu.roll(kr, LANES - MLA_ROPE, axis=1)
    k = (jnp.dot(ckv, wk_ref[...], preferred_element_type=F32)
         + jnp.dot(kr.astype(BF16), place_ref[...], preferred_element_type=F32))
    k_ref[...] = k.astype(BF16)
    vt = lax.dot_general(wvt_ref[...], ckv, (((1,), (1,)), ((), ())), preferred_element_type=F32)
    row = lax.broadcasted_iota(jnp.int32, (MLA_HEADS * VT_ROWS, 1), 0) % VT_ROWS
    vt_ref[0] = (vt + jnp.where(row == MLA_V, 1.0, 0.0)).astype(BF16)


def _mla_proj(p, L, q_norm, wqa, wqb, kv_norm, wk, wvt, place, qc, qs, kt, tm=512):
    t = p.shape[0]
    hw = MLA_HEADS * HEAD_PAD
    nper = L // tm
    fix = lambda i: (0, 0)
    pos = lambda i: (i % nper, 0)
    row = lambda i: (i, 0)
    wide = 6 * LANES
    return pl.pallas_call(
        _mla_proj_body,
        grid=(t // tm,),
        in_specs=[pl.BlockSpec((tm, wide), lambda i: (i, 3 * HY_WIDTH // wide)),
                  pl.BlockSpec((1, MLA_Q_RANK), fix),
                  pl.BlockSpec((MLA_Q_RANK, hw), fix), pl.BlockSpec((MLA_Q_RANK, hw), fix),
                  pl.BlockSpec((1, MLA_KV_RANK), fix),
                  pl.BlockSpec((MLA_KV_RANK, hw), fix),
                  pl.BlockSpec((MLA_HEADS * VT_ROWS, MLA_KV_RANK), fix),
                  pl.BlockSpec((LANES, hw), fix),
                  pl.BlockSpec((tm, HEAD_PAD), pos), pl.BlockSpec((tm, HEAD_PAD), pos),
                  pl.BlockSpec((tm, LANES), pos)],
        out_specs=[pl.BlockSpec((tm, hw), row), pl.BlockSpec((tm, hw), row),
                   pl.BlockSpec((1, MLA_HEADS * VT_ROWS, tm), lambda i: (i // nper, 0, i % nper))],
        out_shape=[jax.ShapeDtypeStruct((t, hw), BF16), jax.ShapeDtypeStruct((t, hw), BF16),
                   jax.ShapeDtypeStruct((t // L, MLA_HEADS * VT_ROWS, L), BF16)],
        compiler_params=_params("parallel"),
        name="mla_proj",
    )(p, q_norm, wqa, wqb, kv_norm, wk, wvt, place, qc, qs, kt)


def _attn_body(q_ref, k_ref, vt_ref, o_ref, *, kc):
    L = k_ref.shape[1]
    nchunk = L // kc
    heads = range(2)
    qs = [q_ref[0, :, h * HEAD_PAD:(h + 1) * HEAD_PAD] for h in heads]

    def scores(h, c):
        return lax.dot_general(k_ref[0, c * kc:(c + 1) * kc, h * HEAD_PAD:(h + 1) * HEAD_PAD], qs[h],
                               (((1,), (1,)), ((), ())), preferred_element_type=F32)

    m = [None, None]
    acc = [None, None]
    s_next = [scores(h, 0) for h in heads]
    for c in range(nchunk):
        s_cur = s_next
        if c + 1 < nchunk:
            s_next = [scores(h, c + 1) for h in heads]
        for h in heads:
            s = s_cur[h]
            cmax = jnp.max(s, axis=0, keepdims=True)
            m_new = cmax if c == 0 else jnp.maximum(m[h], cmax)
            e = jnp.exp2(s - m_new).astype(BF16)
            pv = jnp.dot(vt_ref[0, h * VT_ROWS:(h + 1) * VT_ROWS, c * kc:(c + 1) * kc], e,
                         preferred_element_type=F32)
            acc[h] = pv if c == 0 else acc[h] * jnp.exp2(m[h] - m_new) + pv
            m[h] = m_new
    out_t = jnp.concatenate([acc[h][:MLA_V] / acc[h][MLA_V:MLA_V + 1] for h in heads], axis=0)
    o_ref[0] = out_t.T.astype(o_ref.dtype)


def _attention(q, k, vt, tq=512, kc=512):
    bsz, L, _ = q.shape
    return pl.pallas_call(
        functools.partial(_attn_body, kc=kc),
        grid=(bsz, MLA_HEADS // 2, L // tq),
        in_specs=[pl.BlockSpec((1, tq, 2 * HEAD_PAD), lambda b, g, i: (b, i, g)),
                  pl.BlockSpec((1, L, 2 * HEAD_PAD), lambda b, g, i: (b, 0, g)),
                  pl.BlockSpec((1, 2 * VT_ROWS, L), lambda b, g, i: (b, g, 0))],
        out_specs=pl.BlockSpec((1, tq, 2 * MLA_V), lambda b, g, i: (b, i, g)),
        out_shape=jax.ShapeDtypeStruct((bsz, L, MLA_HEADS * MLA_V), BF16),
        compiler_params=_params("parallel", "parallel", "parallel"),
        name="attention",
    )(q, k, vt)


def _odd_mix_body(um_ref, ul_ref, ur_ref, dw_ref, db_ref, lg_ref, lb_ref, pw_ref, ps_ref,
                  cf_ref, pool_ref, h_ref, u_ref, conv_ref, hp_ref, *, tl, L, rows):
    i = pl.program_id(1)
    last = pl.num_programs(1) - 1

    def glu(u):
        return u[:, :CF_WIDTH] * jax.nn.sigmoid(u[:, CF_WIDTH:2 * CF_WIDTH])

    ul = ul_ref[0]
    um = um_ref[0]
    ur = ur_ref[0]
    h_ref[0:HALO, :] = jnp.where(i == 0, 0.0, glu(ul))
    h_ref[HALO:HALO + tl, :] = glu(um)
    h_ref[HALO + tl:, :] = jnp.where(i == last, 0.0, glu(ur))
    u_ref[0:HALO, :] = jnp.where(i == 0, 0.0, ul[:, 2 * CF_WIDTH:])
    u_ref[HALO:HALO + tl, :] = um[:, 2 * CF_WIDTH:]
    u_ref[HALO + tl:, :] = jnp.where(i == last, 0.0, ur[:, 2 * CF_WIDTH:])

    span = tl + 2 * HALO - SUBLANES
    for j in range(SUBLANES):
        hp_ref[j, 0:span, :] = h_ref[j:j + span, :]

    off = HALO - CF_KERNEL // 2
    for r in range(0, tl, rows):
        for c in range(0, CF_WIDTH, LANES):
            acc = jnp.zeros((rows, LANES), F32) + db_ref[:, c:c + LANES]
            for k in range(CF_KERNEL):
                j = (off + k) % SUBLANES
                base = r + off + k - j
                acc = acc + dw_ref[k:k + 1, c:c + LANES] * hp_ref[j, base:base + rows, c:c + LANES]
            conv_ref[r:r + rows, c:c + LANES] = acc
    hn = _layer_norm(conv_ref[...], lg_ref[...], lb_ref[...])
    cf_ref[0] = (hn * jax.nn.sigmoid(hn)).astype(cf_ref.dtype)

    pos = i * tl + lax.broadcasted_iota(jnp.int32, (tl, 1), 0)
    for gi, w in enumerate(POOL_WINDOWS):
        lo = w // 2
        hi = w - 1 - lo
        c0 = gi * POOL_GROUP
        tot = u_ref[HALO - lo:HALO - lo + tl, c0:c0 + POOL_GROUP]
        for d in range(-lo + 1, hi + 1):
            tot = tot + u_ref[HALO + d:HALO + d + tl, c0:c0 + POOL_GROUP]
        cnt = (jnp.minimum(pos + hi + 1, L) - jnp.maximum(pos - lo, 0)).astype(F32)
        dlt = tot / cnt - u_ref[HALO:HALO + tl, c0:c0 + POOL_GROUP]
        pool_ref[0, :, c0:c0 + POOL_GROUP] = (
            jnp.dot(dlt.astype(BF16), pw_ref[gi], preferred_element_type=F32)
            * ps_ref[:, c0:c0 + POOL_GROUP]).astype(pool_ref.dtype)


def _odd_mix(u, dw_w, dw_b, ln_g, ln_b, pool_w, pool_scale, tl=256, rows=64):
    bsz, L, w = u.shape
    main, left, right = _halo_specs(tl, w, 0)
    fix2 = lambda b, i: (0, 0)
    out_spec = pl.BlockSpec((1, tl, CF_WIDTH), lambda b, i: (b, i, 0))
    return pl.pallas_call(
        functools.partial(_odd_mix_body, tl=tl, L=L, rows=rows),
        grid=(bsz, L // tl),
        in_specs=[pl.BlockSpec((1, tl, w), main),
                  pl.BlockSpec((1, HALO, w), left),
                  pl.BlockSpec((1, HALO, w), functools.partial(right, nblk=L // HALO)),
                  pl.BlockSpec((CF_KERNEL, CF_WIDTH), fix2),
                  pl.BlockSpec((1, CF_WIDTH), fix2), pl.BlockSpec((1, CF_WIDTH), fix2),
                  pl.BlockSpec((1, CF_WIDTH), fix2),
                  pl.BlockSpec((len(POOL_WINDOWS), POOL_GROUP, POOL_GROUP), lambda b, i: (0, 0, 0)),
                  pl.BlockSpec((1, POOL_WIDTH), fix2)],
        out_specs=[out_spec, out_spec],
        out_shape=[jax.ShapeDtypeStruct((bsz, L, CF_WIDTH), BF16)] * 2,
        scratch_shapes=[pltpu.VMEM((tl + 2 * HALO, CF_WIDTH), F32),
                        pltpu.VMEM((tl + 2 * HALO, POOL_WIDTH), F32),
                        pltpu.VMEM((tl, CF_WIDTH), F32),
                        pltpu.VMEM((SUBLANES, tl + 2 * HALO, CF_WIDTH), F32)],
        compiler_params=_params("parallel", "parallel"),
        name="odd_mix",
    )(u, u, u, dw_w, dw_b, ln_g, ln_b, pool_w, pool_scale)


def _rope_tables(L):
    inv = 1.0 / (ROPE_THETA ** (jnp.arange(0, MLA_ROPE, 2, dtype=F32) / MLA_ROPE))
    ang = jnp.arange(L, dtype=F32)[:, None] * inv[None, :]
    cos, sin = jnp.cos(ang), jnp.sin(ang)
    scale = (MLA_NOPE + MLA_ROPE) ** -0.5 * math.log2(math.e)
    ones = jnp.ones((L, MLA_NOPE), F32)
    zeros = jnp.zeros((L, MLA_NOPE), F32)
    pad = jnp.zeros((L, HEAD_PAD - MLA_NOPE - MLA_ROPE), F32)
    qc = scale * jnp.concatenate([ones, cos, cos, pad], axis=1)
    qs = scale * jnp.concatenate([zeros, sin, sin, pad], axis=1)
    kt = jnp.concatenate([cos, cos, sin, sin, jnp.zeros((L, LANES - 2 * MLA_ROPE), F32)], axis=1)
    return qc, qs, kt


def _rot_half_cols(w):
    half = w.shape[-1] // 2
    return jnp.concatenate([-w[..., half:], w[..., :half]], axis=-1)


def _prep_even(ev_w_in, mla_w_uq, mla_w_ukv):
    k = ev_w_in.shape[0]
    base = 3 * HY_WIDTH + MLA_Q_RANK + MLA_KV_RANK
    kr = ev_w_in[:, base:base + MLA_ROPE]
    w_in = jnp.concatenate(
        [ev_w_in[:, :base], kr, _rot_half_cols(kr),
         jnp.zeros((k, LANES - 2 * MLA_ROPE), F32)], axis=1).astype(BF16)
    uq = mla_w_uq.reshape(MLA_Q_RANK, MLA_HEADS, MLA_NOPE + MLA_ROPE)
    zq = jnp.zeros((MLA_Q_RANK, MLA_HEADS, HEAD_PAD - MLA_NOPE - MLA_ROPE), F32)
    wqa = jnp.concatenate([uq, zq], axis=2).reshape(MLA_Q_RANK, -1).astype(BF16)
    wqb = jnp.concatenate([jnp.zeros_like(uq[:, :, :MLA_NOPE]), _rot_half_cols(uq[:, :, MLA_NOPE:]), zq],
                          axis=2).reshape(MLA_Q_RANK, -1).astype(BF16)
    ukv = mla_w_ukv.reshape(MLA_KV_RANK, MLA_HEADS, MLA_NOPE + MLA_V)
    zk = jnp.zeros((MLA_KV_RANK, MLA_HEADS, HEAD_PAD - MLA_NOPE), F32)
    wk = jnp.concatenate([ukv[:, :, :MLA_NOPE], zk], axis=2).reshape(MLA_KV_RANK, -1).astype(BF16)
    zv = jnp.zeros((MLA_KV_RANK, MLA_HEADS, VT_ROWS - MLA_V), F32)
    wvt = jnp.concatenate([ukv[:, :, MLA_NOPE:], zv], axis=2).reshape(MLA_KV_RANK, -1).T.astype(BF16)
    return w_in, wqa, wqb, wk, wvt


def _rope_place():
    e = np.zeros((LANES, MLA_HEADS * HEAD_PAD), np.float32)
    for h in range(MLA_HEADS):
        for j in range(MLA_ROPE):
            e[j, h * HEAD_PAD + MLA_NOPE + j] = 1.0
    return jnp.asarray(e, BF16)


def _trunk(x, w, filt):
    bsz, L, d = x.shape
    t = bsz * L
    xf = x.reshape(t, d)
    qc, qs, kt = _rope_tables(L)
    place = _rope_place()
    zero_b = jnp.zeros((1, d), F32)
    for layer in range(DEPTH):
        i = layer // 2
        if layer % 2 == 0:
            w_in, wqa, wqb, wk, wvt = w["even"][i]
            p = _linear(xf, w_in, jnp.zeros((1, EVEN_COLS), F32))
            z, x0c = _hy_gate(p.reshape(bsz, L, EVEN_COLS), w["hy_conv_w"][i], w["hy_conv_b"][i].reshape(1, -1))
            gr, gi = filt[(L, i)]
            m1 = _hy_conv(z, x0c, gr, gi, w["hy_skip"][i], L).reshape(t, HY_WIDTH)
            q, k, vt = _mla_proj(p, L, w["mla_q_norm"][i].reshape(1, -1), wqa, wqb,
                                 w["mla_kv_norm"][i].reshape(1, -1), wk, wvt, place, qc, qs, kt)
            hw = MLA_HEADS * HEAD_PAD
            m2 = _attention(q.reshape(bsz, L, hw), k.reshape(bsz, L, hw), vt).reshape(t, MLA_HEADS * MLA_V)
            w_out = w["ev_w_out"][i]
            b_out = zero_b
        else:
            u = _linear(xf, w["od_w_in"][i], w["od_b_in"][i].reshape(1, -1))
            cf, pool = _odd_mix(u.reshape(bsz, L, -1), w["cf_dw_w"][i], w["cf_dw_b"][i].reshape(1, -1),
                                w["cf_ln_g"][i].reshape(1, -1), w["cf_ln_b"][i].reshape(1, -1),
                                w["pool_w"][i], w["pool_scale"][i].reshape(1, -1))
            m1 = cf.reshape(t, CF_WIDTH)
            m2 = pool.reshape(t, POOL_WIDTH)
            w_out = w["od_w_out"][i]
            b_out = w["od_b_out"][i].reshape(1, -1)
        half = w_out.shape[0] // 2
        xf = _layer_tail(m1, m2, xf, w_out[:half], w_out[half:], b_out,
                         w["ln1_g"][layer].reshape(1, -1), w["ln1_b"][layer].reshape(1, -1),
                         w["mlp_w1"][layer], w["mlp_w2"][layer],
                         w["ln2_g"][layer].reshape(1, -1), w["ln2_b"][layer].reshape(1, -1))
    return xf.reshape(bsz, L, d)


def kernel(x_prompt, x_sample, ev_w_in, hy_conv_w, hy_conv_b, hy_filt_w1, hy_filt_b1, hy_filt_w_inner, hy_filt_b_inner, hy_filt_freq, hy_filt_w_out, hy_skip, mla_q_norm, mla_w_uq, mla_kv_norm, mla_w_ukv, ev_w_out, od_w_in, od_b_in, cf_dw_w, cf_dw_b, cf_ln_g, cf_ln_b, pool_w, pool_scale, od_w_out, od_b_out, ln1_g, ln1_b, mlp_w1, mlp_w2, ln2_g, ln2_b):
    n_even = ev_w_in.shape[0]
    w = {
        "even": [_prep_even(ev_w_in[i], mla_w_uq[i], mla_w_ukv[i]) for i in range(n_even)],
        "hy_conv_w": hy_conv_w, "hy_conv_b": hy_conv_b, "hy_skip": hy_skip,
        "mla_q_norm": mla_q_norm, "mla_kv_norm": mla_kv_norm,
        "ev_w_out": ev_w_out.astype(BF16),
        "od_w_in": od_w_in.astype(BF16), "od_b_in": od_b_in,
        "cf_dw_w": cf_dw_w, "cf_dw_b": cf_dw_b, "cf_ln_g": cf_ln_g, "cf_ln_b": cf_ln_b,
        "pool_w": pool_w.astype(BF16), "pool_scale": pool_scale,
        "od_w_out": od_w_out.astype(BF16), "od_b_out": od_b_out,
        "ln1_g": ln1_g, "ln1_b": ln1_b, "ln2_g": ln2_g, "ln2_b": ln2_b,
        "mlp_w1": mlp_w1.astype(BF16), "mlp_w2": mlp_w2.astype(BF16),
    }
    filt = {}
    for L in sorted({x_prompt.shape[1], x_sample.shape[1]}):
        for i in range(n_even):
            g = _hy_filter(L, hy_filt_w1[i], hy_filt_b1[i], hy_filt_w_inner[i], hy_filt_b_inner[i],
                           hy_filt_freq[i], hy_filt_w_out[i])
            filt[(L, i)] = _hy_spectrum(g, L)
    return (_trunk(x_prompt, w, filt), _trunk(x_sample, w, filt))
```

```python
import functools
import math

import numpy as np
import jax
import jax.numpy as jnp
from jax import lax
from jax.experimental import pallas as pl
from jax.experimental.pallas import tpu as pltpu

F32 = jnp.float32
BF16 = jnp.bfloat16

D_MODEL = 1024
DEPTH = 4
HY_WIDTH = D_MODEL // 2
HY_EMB = 33
HY_BANDS = (HY_EMB - 1) // 2
HY_FILT = 64
HY_INNER = 2
HY_MAX_DECAY = math.log(1e-2) / 0.3
HY_MIN_DECAY = math.log(1e-2) / 1.5
MLA_HEADS = 8
MLA_NOPE = 64
MLA_ROPE = 32
MLA_V = 64
MLA_Q_RANK = 384
MLA_KV_RANK = 256
ROPE_THETA = 10000.0
CF_WIDTH = D_MODEL // 2
CF_KERNEL = 31
POOL_WIDTH = D_MODEL // 2
POOL_WINDOWS = (2, 4, 8, 16)
POOL_GROUP = POOL_WIDTH // len(POOL_WINDOWS)
D_FF = 4 * D_MODEL
DN_ALPHA = (2 * DEPTH) ** 0.25
LN_EPS = 1e-5
RMS_EPS = 1e-6

LANES = 128
SUBLANES = 8
VMEM_LIMIT = 52 * 1024 * 1024
HEAD_PAD = 128
VT_ROWS = 80
EVEN_COLS = 3 * HY_WIDTH + 6 * LANES
FFT_N2 = 32
FFT_PITCH = FFT_N2 + SUBLANES
FFT_MAX_UNROLL = 18
HALO = 16
ODD_STAGES = 3


def _params(*sem):
    return pltpu.CompilerParams(dimension_semantics=sem, vmem_limit_bytes=VMEM_LIMIT)


def _layer_norm(r, g, b):
    mu = jnp.mean(r, axis=-1, keepdims=True)
    d = r - mu
    var = jnp.mean(d * d, axis=-1, keepdims=True)
    return d * lax.rsqrt(var + LN_EPS) * g + b


def _bdot(a, b):
    return jnp.dot(a.astype(BF16), b.astype(BF16), preferred_element_type=F32)


def _layer_tail_body(m1_ref, m2_ref, x_ref, wo1_ref, wo2_ref, bo_ref, g1_ref, b1_ref,
                     w1_ref, w2_ref, g2_ref, b2_ref, o_ref, *, ff_chunk, streams):
    rows = x_ref.shape[0] // streams
    sls = [slice(s * rows, (s + 1) * rows) for s in range(streams)]
    x1 = []
    for sl in sls:
        y = (jnp.dot(m1_ref[sl, :], wo1_ref[...], preferred_element_type=F32)
             + jnp.dot(m2_ref[sl, :], wo2_ref[...], preferred_element_type=F32) + bo_ref[...])
        x1.append(_layer_norm(DN_ALPHA * x_ref[sl, :] + y, g1_ref[...], b1_ref[...]))
    for sl, xs in zip(sls, x1):
        xb = xs.astype(BF16)
        acc = DN_ALPHA * xs
        for c in range(D_FF // ff_chunk):
            h = jnp.dot(xb, w1_ref[:, c * ff_chunk:(c + 1) * ff_chunk], preferred_element_type=F32)
            h = jnp.maximum(h, 0.0)
            acc = acc + jnp.dot((h * h).astype(BF16), w2_ref[c * ff_chunk:(c + 1) * ff_chunk, :],
                                preferred_element_type=F32)
        o_ref[sl, :] = _layer_norm(acc, g2_ref[...], b2_ref[...])


def _layer_tail(m1, m2, x, wo1, wo2, bo, g1, b1, w1, w2, g2, b2, tm=512, ff_chunk=1024, streams=2):
    t, d = x.shape
    k = m1.shape[1]
    row = lambda i: (i, 0)
    fix = lambda i: (0, 0)
    resident = lambda shape: pl.BlockSpec(shape, fix, pipeline_mode=pl.Buffered(1))
    vec = pl.BlockSpec((1, d), fix)
    return pl.pallas_call(
        functools.partial(_layer_tail_body, ff_chunk=ff_chunk, streams=streams),
        grid=(t // tm,),
        in_specs=[pl.BlockSpec((tm, k), row), pl.BlockSpec((tm, k), row), pl.BlockSpec((tm, d), row),
                  resident((k, d)), resident((k, d)), vec, vec, vec,
                  resident((d, D_FF)), resident((D_FF, d)), vec, vec],
        out_specs=pl.BlockSpec((tm, d), row),
        out_shape=jax.ShapeDtypeStruct((t, d), F32),
        compiler_params=_params("parallel"),
        name="layer_tail",
    )(m1, m2, x, wo1, wo2, bo, g1, b1, w1, w2, g2, b2)


def _halo_specs(tl, L):
    per = tl // HALO
    nblk = L // HALO

    def main(b, i):
        return (b, i, 0)

    def left(b, i):
        return (b, jnp.maximum(i * per - 1, 0), 0)

    def right(b, i):
        return (b, jnp.minimum((i + 1) * per, nblk - 1), 0)

    return main, left, right


def _even_in_body(xm_ref, xl_ref, xr_ref, w_ref, cw_ref, cb_ref, z_ref, x0_ref, pm_ref, ext_ref, *, tl):
    i = pl.program_id(1)
    last = pl.num_programs(1) - 1
    hw = 3 * HY_WIDTH
    xm = xm_ref[0].astype(BF16)
    xl = xl_ref[0].astype(BF16)
    xr = xr_ref[0].astype(BF16)

    def project(part):
        cols = slice(part * HY_WIDTH, (part + 1) * HY_WIDTH)
        ext_ref[0:HALO, cols] = jnp.where(i == 0, 0.0, _bdot(xl, w_ref[:, cols]))
        ext_ref[HALO:HALO + tl, cols] = _bdot(xm, w_ref[:, cols])
        ext_ref[HALO + tl:, cols] = jnp.where(i == last, 0.0, _bdot(xr, w_ref[:, cols]))

    def conv(part):
        cols = slice(part * HY_WIDTH, (part + 1) * HY_WIDTH)
        return (cw_ref[0:1, cols] * ext_ref[HALO - 1:HALO - 1 + tl, cols]
                + cw_ref[1:2, cols] * ext_ref[HALO:HALO + tl, cols]
                + cw_ref[2:3, cols] * ext_ref[HALO + 1:HALO + 1 + tl, cols] + cb_ref[:, cols])

    project(0)
    project(1)
    x0_ref[0] = conv(0)
    project(2)
    x1c = conv(1)
    pm_ref[0] = _bdot(xm, w_ref[:, hw:])
    z_ref[0] = x1c * conv(2)


def _even_in(x, w_in, conv_w, conv_b, tl=512):
    bsz, L, d = x.shape
    hw = 3 * HY_WIDTH
    rest = EVEN_COLS - hw
    main, left, right = _halo_specs(tl, L)
    fix = lambda b, i: (0, 0)
    hy_spec = pl.BlockSpec((1, tl, HY_WIDTH), main)
    return pl.pallas_call(
        functools.partial(_even_in_body, tl=tl),
        grid=(bsz, L // tl),
        in_specs=[pl.BlockSpec((1, tl, d), main), pl.BlockSpec((1, HALO, d), left),
                  pl.BlockSpec((1, HALO, d), right),
                  pl.BlockSpec((d, EVEN_COLS), fix), pl.BlockSpec((3, hw), fix), pl.BlockSpec((1, hw), fix)],
        out_specs=[hy_spec, hy_spec, pl.BlockSpec((1, tl, rest), main)],
        out_shape=[jax.ShapeDtypeStruct((bsz, L, HY_WIDTH), F32)] * 2
        + [jax.ShapeDtypeStruct((bsz, L, rest), F32)],
        scratch_shapes=[pltpu.VMEM((tl + 2 * HALO, hw), F32)],
        compiler_params=_params("parallel", "parallel"),
        name="even_in",
    )(x, x, x, w_in, conv_w, conv_b)


def _hy_filter_body(w1_ref, b1_ref, wi_ref, bi_ref, fr_ref, wo_ref, g_ref, *, L, tr):
    hi = lax.Precision.HIGHEST
    m = pl.program_id(0) * tr + lax.broadcasted_iota(jnp.int32, (tr, 1), 0)
    j = jnp.where(m < L, m, 2 * L - m).astype(F32)
    t = j * (1.0 / (L - 1))
    ang = j * (2.0 * math.pi / L)
    band_step = (HY_BANDS - 1 - 1e-4) / (HY_BANDS - 1)
    bands = 1e-4 + band_step * lax.broadcasted_iota(jnp.int32, (1, HY_BANDS), 1).astype(F32)
    arg = ang * bands
    freq = fr_ref[...]
    pre = (t * w1_ref[0:1, :]
           + jnp.dot(jnp.cos(arg), w1_ref[1:1 + HY_BANDS, :], precision=hi, preferred_element_type=F32)
           - jnp.dot(jnp.sin(arg), w1_ref[1 + HY_BANDS:, :], precision=hi, preferred_element_type=F32)
           + b1_ref[...])
    h = jnp.sin(freq * pre)
    for l in range(HY_INNER):
        h = jnp.sin(freq * (jnp.dot(h, wi_ref[l], precision=hi, preferred_element_type=F32) + bi_ref[l]))
    k = jnp.dot(h, wo_ref[...], precision=hi, preferred_element_type=F32)
    ch = lax.broadcasted_iota(jnp.int32, (1, HY_WIDTH), 1).astype(F32)
    deltas = jnp.abs(HY_MIN_DECAY + ch * ((HY_MAX_DECAY - HY_MIN_DECAY) / (HY_WIDTH - 1)))
    decay = jnp.exp(-t * deltas)
    taps = jnp.where(m < L, k[:, :HY_WIDTH], k[:, HY_WIDTH:]) * decay
    g_ref[...] = jnp.where(m == L, 0.0, taps)


def _hy_filter(L, w1, b1, w_inner, b_inner, freq, w_out, tr=512):
    n = 2 * L
    full = lambda *shape: pl.BlockSpec(shape, lambda i: (0,) * len(shape))
    return pl.pallas_call(
        functools.partial(_hy_filter_body, L=L, tr=tr),
        grid=(n // tr,),
        in_specs=[full(HY_EMB, HY_FILT), full(1, HY_FILT), full(HY_INNER, HY_FILT, HY_FILT),
                  full(HY_INNER, 1, HY_FILT), full(1, HY_FILT), full(HY_FILT, 2 * HY_WIDTH)],
        out_specs=pl.BlockSpec((tr, HY_WIDTH), lambda i: (i, 0)),
        out_shape=jax.ShapeDtypeStruct((n, HY_WIDTH), F32),
        compiler_params=_params("parallel"),
        name="hy_filter",
    )(w1, b1.reshape(1, -1), w_inner, b_inner.reshape(HY_INNER, 1, HY_FILT), freq.reshape(1, -1), w_out)


def _fft_dims(L):
    n = 2 * L
    n1 = n // FFT_N2
    return n, n1, n1 // 2 + SUBLANES, n1 // 2


@functools.lru_cache(maxsize=None)
def _fft_tables(L):
    n, n1, kh, h1 = _fft_dims(L)
    n2 = FFT_N2
    k1 = np.arange(kh, dtype=np.float64)[:, None]
    t1 = np.arange(n1, dtype=np.float64)[None, :]
    th = 2.0 * np.pi * k1 * t1 / n1
    fy = np.concatenate([np.cos(th), -np.sin(th)], axis=0)
    k2 = np.arange(n2, dtype=np.float64)[None, :, None]
    t2 = np.arange(n2, dtype=np.float64)[None, None, :]
    ph = 2.0 * np.pi * (k2 * t2 / n2 + k1[:, :, None] * t2 / n)
    mr, mi = np.cos(ph), -np.sin(ph)
    mx = np.concatenate([np.concatenate([mr, -mi], axis=2),
                         np.concatenate([mi, mr], axis=2)], axis=1)
    mxi = np.transpose(mx, (0, 2, 1))
    wk = np.zeros((kh,), np.float64)
    wk[:n1 // 2 + 1] = 2.0
    wk[0] = 1.0
    wk[n1 // 2] = 1.0
    thi = th[:, :h1].T
    fyi = np.concatenate([np.cos(thi) * wk[None, :], -np.sin(thi) * wk[None, :]], axis=1)
    return fy, mx, mxi, fyi


def _fft_unroll(kh):
    return max(u for u in range(1, FFT_MAX_UNROLL + 1) if kh % u == 0)


def _pitch_rows(src_ref, dst_ref, nblk, batch_idx=None):
    def body(t1, carry):
        s = pl.multiple_of(t1 * FFT_N2, SUBLANES)
        d = pl.multiple_of(t1 * FFT_PITCH, SUBLANES)
        if batch_idx is None:
            dst_ref[pl.ds(d, FFT_N2), :] = src_ref[pl.ds(s, FFT_N2), :]
        else:
            dst_ref[pl.ds(d, FFT_N2), :] = src_ref[batch_idx, pl.ds(s, FFT_N2), :]
        return carry
    lax.fori_loop(0, nblk, body, 0)


def _fft_stage_y(src_ref, fy_ref, ar_ref, ai_ref, nblk, kh):
    fy = fy_ref[...]
    for t2 in range(FFT_N2):
        x = src_ref[pl.ds(t2, nblk, stride=FFT_PITCH), :].astype(BF16)
        a = jnp.dot(fy, x, preferred_element_type=F32)
        ar_ref[pl.ds(t2, kh, stride=FFT_PITCH), :] = a[:kh]
        ai_ref[pl.ds(t2, kh, stride=FFT_PITCH), :] = a[kh:]


def _hy_spectrum_body(g_ref, fy_ref, mx_ref, gr_ref, gi_ref, gs_ref, ar_ref, ai_ref, *, L):
    n, n1, kh, _ = _fft_dims(L)
    _pitch_rows(g_ref, gs_ref, n1)
    _fft_stage_y(gs_ref, fy_ref, ar_ref, ai_ref, n1, kh)
    unroll = _fft_unroll(kh)

    def body(it, carry):
        k1s = [it * unroll + u for u in range(unroll)]
        a = []
        for k1 in k1s:
            src = pl.multiple_of(k1 * FFT_PITCH, SUBLANES)
            a.append(jnp.concatenate([ar_ref[pl.ds(src, FFT_N2), :], ai_ref[pl.ds(src, FFT_N2), :]],
                                     axis=0).astype(BF16))
        z = [jnp.dot(mx_ref[k1], x, preferred_element_type=F32) * (1.0 / n) for k1, x in zip(k1s, a)]
        for k1, zz in zip(k1s, z):
            dst = pl.multiple_of(k1 * FFT_N2, SUBLANES)
            gr_ref[pl.ds(dst, FFT_N2), :] = zz[:FFT_N2]
            gi_ref[pl.ds(dst, FFT_N2), :] = zz[FFT_N2:]
        return carry
    lax.fori_loop(0, kh // unroll, body, 0)


def _hy_spectrum(g, L):
    n, n1, kh, _ = _fft_dims(L)
    fy, mx, _, _ = _fft_tables(L)
    out_spec = pl.BlockSpec((kh * FFT_N2, LANES), lambda c: (0, c))
    return pl.pallas_call(
        functools.partial(_hy_spectrum_body, L=L),
        grid=(HY_WIDTH // LANES,),
        in_specs=[pl.BlockSpec((n, LANES), lambda c: (0, c)),
                  pl.BlockSpec((2 * kh, n1), lambda c: (0, 0)),
                  pl.BlockSpec((kh, 2 * FFT_N2, 2 * FFT_N2), lambda c: (0, 0, 0))],
        out_specs=[out_spec, out_spec],
        out_shape=[jax.ShapeDtypeStruct((kh * FFT_N2, HY_WIDTH), F32)] * 2,
        scratch_shapes=[pltpu.VMEM((n1 * FFT_PITCH, LANES), F32),
                        pltpu.VMEM((kh * FFT_PITCH, LANES), F32),
                        pltpu.VMEM((kh * FFT_PITCH, LANES), F32)],
        compiler_params=_params("parallel"),
        name="hy_spectrum",
    )(g, jnp.asarray(fy, BF16), jnp.asarray(mx, BF16))


def _hy_conv_body(z_ref, x0_ref, gr_ref, gi_ref, fy_ref, mx_ref, mxi_ref, fyi_ref, skip_ref, o_ref,
                  zs_ref, ar_ref, ai_ref, *, L):
    n, n1, kh, h1 = _fft_dims(L)
    _pitch_rows(z_ref, zs_ref, h1, batch_idx=0)
    _fft_stage_y(zs_ref, fy_ref, ar_ref, ai_ref, h1, kh)

    unroll = _fft_unroll(kh)

    def spectral(it, carry):
        k1s = [it * unroll + u for u in range(unroll)]
        rows = [pl.multiple_of(k1 * FFT_PITCH, SUBLANES) for k1 in k1s]
        grows = [pl.multiple_of(k1 * FFT_N2, SUBLANES) for k1 in k1s]
        a = [jnp.concatenate([ar_ref[pl.ds(r, FFT_N2), :], ai_ref[pl.ds(r, FFT_N2), :]], axis=0).astype(BF16)
             for r in rows]
        z = [jnp.dot(mx_ref[k1], x, preferred_element_type=F32) for k1, x in zip(k1s, a)]
        y = []
        for zz, g in zip(z, grows):
            zr, zi = zz[:FFT_N2], zz[FFT_N2:]
            gr = gr_ref[pl.ds(g, FFT_N2), :]
            gi = gi_ref[pl.ds(g, FFT_N2), :]
            y.append(jnp.concatenate([zr * gr - zi * gi, zr * gi + zi * gr], axis=0).astype(BF16))
        b = [jnp.dot(mxi_ref[k1], x, preferred_element_type=F32) for k1, x in zip(k1s, y)]
        for r, bb in zip(rows, b):
            ar_ref[pl.ds(r, FFT_N2), :] = bb[:FFT_N2]
            ai_ref[pl.ds(r, FFT_N2), :] = bb[FFT_N2:]
        return carry
    lax.fori_loop(0, kh // unroll, spectral, 0)

    fyi = fyi_ref[...]
    for t2 in range(FFT_N2):
        b = jnp.concatenate([ar_ref[pl.ds(t2, kh, stride=FFT_PITCH), :],
                             ai_ref[pl.ds(t2, kh, stride=FFT_PITCH), :]], axis=0)
        zs_ref[pl.ds(t2, h1, stride=FFT_PITCH), :] = jnp.dot(fyi, b.astype(BF16), preferred_element_type=F32)

    skip = skip_ref[...]

    def finish(t1, carry):
        s = pl.multiple_of(t1 * FFT_N2, SUBLANES)
        d = pl.multiple_of(t1 * FFT_PITCH, SUBLANES)
        y = zs_ref[pl.ds(d, FFT_N2), :] + z_ref[0, pl.ds(s, FFT_N2), :] * skip
        o_ref[0, pl.ds(s, FFT_N2), :] = (x0_ref[0, pl.ds(s, FFT_N2), :] * y).astype(o_ref.dtype)
        return carry
    lax.fori_loop(0, h1, finish, 0)


def _hy_conv(z, x0c, gr, gi, skip, L):
    bsz = z.shape[0]
    n, n1, kh, h1 = _fft_dims(L)
    fy, mx, mxi, fyi = _fft_tables(L)
    seq = pl.BlockSpec((1, L, LANES), lambda c, b: (b, 0, c))
    spec = pl.BlockSpec((kh * FFT_N2, LANES), lambda c, b: (0, c))
    mat = pl.BlockSpec((kh, 2 * FFT_N2, 2 * FFT_N2), lambda c, b: (0, 0, 0))
    return pl.pallas_call(
        functools.partial(_hy_conv_body, L=L),
        grid=(HY_WIDTH // LANES, bsz),
        in_specs=[seq, seq, spec, spec,
                  pl.BlockSpec((2 * kh, h1), lambda c, b: (0, 0)),
                  mat, mat,
                  pl.BlockSpec((h1, 2 * kh), lambda c, b: (0, 0)),
                  pl.BlockSpec((1, LANES), lambda c, b: (0, c))],
        out_specs=seq,
        out_shape=jax.ShapeDtypeStruct((bsz, L, HY_WIDTH), BF16),
        scratch_shapes=[pltpu.VMEM((h1 * FFT_PITCH, LANES), F32),
                        pltpu.VMEM((kh * FFT_PITCH, LANES), F32),
                        pltpu.VMEM((kh * FFT_PITCH, LANES), F32)],
        compiler_params=_params("parallel", "parallel"),
        name="hy_conv",
    )(z, x0c, gr, gi, jnp.asarray(fy[:, :h1], BF16), jnp.asarray(mx, BF16), jnp.asarray(mxi, BF16),
      jnp.asarray(fyi, BF16), skip.reshape(1, -1))


def _rms(x, g):
    return x * lax.rsqrt(jnp.mean(x * x, axis=-1, keepdims=True) + RMS_EPS) * g


def _mla_proj_body(p_ref, qn_ref, wqa_ref, wqb_ref, kvn_ref, wk_ref, wvt_ref, place_ref,
                   qc_ref, qs_ref, kt_ref, q_ref, k_ref, vt_ref):
    p = p_ref[...]
    cq = _rms(p[:, :MLA_Q_RANK], qn_ref[...]).astype(BF16)
    ckv = _rms(p[:, MLA_Q_RANK:MLA_Q_RANK + MLA_KV_RANK], kvn_ref[...]).astype(BF16)
    qa = jnp.dot(cq, wqa_ref[...], preferred_element_type=F32)
    qb = jnp.dot(cq, wqb_ref[...], preferred_element_type=F32)
    qc = qc_ref[...]
    qs = qs_ref[...]
    for h in range(MLA_HEADS):
        sl = slice(h * HEAD_PAD, (h + 1) * HEAD_PAD)
        q_ref[:, sl] = (qa[:, sl] * qc + qb[:, sl] * qs).astype(BF16)
    kr = p[:, MLA_Q_RANK + MLA_KV_RANK:] * kt_ref[...]
    kr = kr + pltpu.roll(kr, LANES - MLA_ROPE, axis=1)
    k = (jnp.dot(ckv, wk_ref[...], preferred_element_type=F32)
         + jnp.dot(kr.astype(BF16), place_ref[...], preferred_element_type=F32))
    k_ref[...] = k.astype(BF16)
    vt = lax.dot_general(wvt_ref[...], ckv, (((1,), (1,)), ((), ())), preferred_element_type=F32)
    row = lax.broadcasted_iota(jnp.int32, (MLA_HEADS * VT_ROWS, 1), 0) % VT_ROWS
    vt_ref[0] = (vt + jnp.where(row == MLA_V, 1.0, 0.0)).astype(BF16)


def _mla_proj(p, L, q_norm, wqa, wqb, kv_norm, wk, wvt, place, qc, qs, kt, tm=512):
    t = p.shape[0]
    hw = MLA_HEADS * HEAD_PAD
    nper = L // tm
    fix = lambda i: (0, 0)
    pos = lambda i: (i % nper, 0)
    row = lambda i: (i, 0)
    wide = 6 * LANES
    return pl.pallas_call(
        _mla_proj_body,
        grid=(t // tm,),
        in_specs=[pl.BlockSpec((tm, wide), row),
                  pl.BlockSpec((1, MLA_Q_RANK), fix),
                  pl.BlockSpec((MLA_Q_RANK, hw), fix), pl.BlockSpec((MLA_Q_RANK, hw), fix),
                  pl.BlockSpec((1, MLA_KV_RANK), fix),
                  pl.BlockSpec((MLA_KV_RANK, hw), fix),
                  pl.BlockSpec((MLA_HEADS * VT_ROWS, MLA_KV_RANK), fix),
                  pl.BlockSpec((LANES, hw), fix),
                  pl.BlockSpec((tm, HEAD_PAD), pos), pl.BlockSpec((tm, HEAD_PAD), pos),
                  pl.BlockSpec((tm, LANES), pos)],
        out_specs=[pl.BlockSpec((tm, hw), row), pl.BlockSpec((tm, hw), row),
                   pl.BlockSpec((1, MLA_HEADS * VT_ROWS, tm), lambda i: (i // nper, 0, i % nper))],
        out_shape=[jax.ShapeDtypeStruct((t, hw), BF16), jax.ShapeDtypeStruct((t, hw), BF16),
                   jax.ShapeDtypeStruct((t // L, MLA_HEADS * VT_ROWS, L), BF16)],
        compiler_params=_params("parallel"),
        name="mla_proj",
    )(p, q_norm, wqa, wqb, kv_norm, wk, wvt, place, qc, qs, kt)


def _attn_body(q_ref, k_ref, vt_ref, o_ref, *, kc):
    L = k_ref.shape[1]
    nchunk = L // kc
    heads = range(2)
    qs = [q_ref[0, :, h * HEAD_PAD:(h + 1) * HEAD_PAD] for h in heads]

    def scores(h, c):
        return lax.dot_general(k_ref[0, c * kc:(c + 1) * kc, h * HEAD_PAD:(h + 1) * HEAD_PAD], qs[h],
                               (((1,), (1,)), ((), ())), preferred_element_type=F32)

    m = [None, None]
    acc = [None, None]
    s_next = [scores(h, 0) for h in heads]
    for c in range(nchunk):
        s_cur = s_next
        if c + 1 < nchunk:
            s_next = [scores(h, c + 1) for h in heads]
        for h in heads:
            s = s_cur[h]
            cmax = jnp.max(s, axis=0, keepdims=True)
            m_new = cmax if c == 0 else jnp.maximum(m[h], cmax)
            e = jnp.exp2(s - m_new).astype(BF16)
            pv = jnp.dot(vt_ref[0, h * VT_ROWS:(h + 1) * VT_ROWS, c * kc:(c + 1) * kc], e,
                         preferred_element_type=F32)
            acc[h] = pv if c == 0 else acc[h] * jnp.exp2(m[h] - m_new) + pv
            m[h] = m_new
    out_t = jnp.concatenate([acc[h][:MLA_V] / acc[h][MLA_V:MLA_V + 1] for h in heads], axis=0)
    o_ref[0] = out_t.T.astype(o_ref.dtype)


def _attention(q, k, vt, tq=1024, kc=256):
    bsz, L, _ = q.shape
    return pl.pallas_call(
        functools.partial(_attn_body, kc=kc),
        grid=(bsz, MLA_HEADS // 2, L // tq),
        in_specs=[pl.BlockSpec((1, tq, 2 * HEAD_PAD), lambda b, g, i: (b, i, g)),
                  pl.BlockSpec((1, L, 2 * HEAD_PAD), lambda b, g, i: (b, 0, g)),
                  pl.BlockSpec((1, 2 * VT_ROWS, L), lambda b, g, i: (b, g, 0))],
        out_specs=pl.BlockSpec((1, tq, 2 * MLA_V), lambda b, g, i: (b, i, g)),
        out_shape=jax.ShapeDtypeStruct((bsz, L, MLA_HEADS * MLA_V), BF16),
        compiler_params=_params("parallel", "parallel", "parallel"),
        name="attention",
    )(q, k, vt)


def _odd_mix_body(xm_ref, xl_ref, xr_ref, w_ref, b_ref, dw_ref, db_ref, lg_ref, lb_ref, pw_ref, ps_ref,
                  cf_ref, pool_ref, h_ref, u_ref, conv_ref, hp_ref, *, tl, L, rows):
    i = pl.program_id(1)
    last = pl.num_programs(1) - 1

    def glu(u):
        return u[:, :CF_WIDTH] * jax.nn.sigmoid(u[:, CF_WIDTH:2 * CF_WIDTH])

    ext = tl + 2 * HALO
    stage = ext // ODD_STAGES

    def x_rows(lo, hi):
        pieces = []
        if lo < HALO:
            pieces.append(xl_ref[0, lo:min(hi, HALO), :])
        if max(lo, HALO) < min(hi, HALO + tl):
            pieces.append(xm_ref[0, max(lo, HALO) - HALO:min(hi, HALO + tl) - HALO, :])
        if hi > HALO + tl:
            pieces.append(xr_ref[0, max(lo, HALO + tl) - HALO - tl:hi - HALO - tl, :])
        return pieces[0] if len(pieces) == 1 else jnp.concatenate(pieces, axis=0)

    def project(s):
        lo = s * stage
        u = _bdot(x_rows(lo, lo + stage), w_ref[...]) + b_ref[...]
        e = lo + lax.broadcasted_iota(jnp.int32, (stage, 1), 0)
        outside = ((e < HALO) & (i == 0)) | ((e >= HALO + tl) & (i == last))
        u = jnp.where(outside, 0.0, u)
        h_ref[lo:lo + stage, :] = glu(u)
        u_ref[lo:lo + stage, :] = u[:, 2 * CF_WIDTH:]

    def phase_copies(s):
        lo = max(s * stage - SUBLANES, 0)
        hi = (s + 1) * stage - SUBLANES
        for j in range(SUBLANES):
            hp_ref[j, lo:hi, :] = h_ref[lo + j:hi + j, :]
        return hi

    off = HALO - CF_KERNEL // 2

    def conv_rows(r):
        for c in range(0, CF_WIDTH, LANES):
            acc = jnp.zeros((rows, LANES), F32) + db_ref[:, c:c + LANES]
            for k in range(CF_KERNEL):
                j = (off + k) % SUBLANES
                base = r + off + k - j
                acc = acc + dw_ref[k:k + 1, c:c + LANES] * hp_ref[j, base:base + rows, c:c + LANES]
            conv_ref[r:r + rows, c:c + LANES] = acc

    reach = rows + (off + CF_KERNEL - 1) // SUBLANES * SUBLANES
    r_next = 0
    project(0)
    for s in range(ODD_STAGES):
        ready = phase_copies(s)
        if s + 1 < ODD_STAGES:
            project(s + 1)
        while r_next < tl and r_next + reach <= ready:
            conv_rows(r_next)
            r_next += rows
    hn = _layer_norm(conv_ref[...], lg_ref[...], lb_ref[...])
    cf_ref[0] = (hn * jax.nn.sigmoid(hn)).astype(cf_ref.dtype)

    pos = i * tl + lax.broadcasted_iota(jnp.int32, (tl, 1), 0)
    for gi, w in enumerate(POOL_WINDOWS):
        lo = w // 2
        hi = w - 1 - lo
        c0 = gi * POOL_GROUP
        tot = u_ref[HALO - lo:HALO - lo + tl, c0:c0 + POOL_GROUP]
        for d in range(-lo + 1, hi + 1):
            tot = tot + u_ref[HALO + d:HALO + d + tl, c0:c0 + POOL_GROUP]
        cnt = (jnp.minimum(pos + hi + 1, L) - jnp.maximum(pos - lo, 0)).astype(F32)
        dlt = tot / cnt - u_ref[HALO:HALO + tl, c0:c0 + POOL_GROUP]
        pool_ref[0, :, c0:c0 + POOL_GROUP] = (
            jnp.dot(dlt.astype(BF16), pw_ref[gi], preferred_element_type=F32)
            * ps_ref[:, c0:c0 + POOL_GROUP]).astype(pool_ref.dtype)


def _odd_mix(x, w_in, b_in, dw_w, dw_b, ln_g, ln_b, pool_w, pool_scale, tl=256, rows=64):
    bsz, L, d = x.shape
    cols = w_in.shape[1]
    assert (tl + 2 * HALO) % (ODD_STAGES * SUBLANES) == 0 and tl % rows == 0
    main, left, right = _halo_specs(tl, L)
    fix2 = lambda b, i: (0, 0)
    out_spec = pl.BlockSpec((1, tl, CF_WIDTH), main)
    return pl.pallas_call(
        functools.partial(_odd_mix_body, tl=tl, L=L, rows=rows),
        grid=(bsz, L // tl),
        in_specs=[pl.BlockSpec((1, tl, d), main),
                  pl.BlockSpec((1, HALO, d), left),
                  pl.BlockSpec((1, HALO, d), right),
                  pl.BlockSpec((d, cols), fix2), pl.BlockSpec((1, cols), fix2),
                  pl.BlockSpec((CF_KERNEL, CF_WIDTH), fix2),
                  pl.BlockSpec((1, CF_WIDTH), fix2), pl.BlockSpec((1, CF_WIDTH), fix2),
                  pl.BlockSpec((1, CF_WIDTH), fix2),
                  pl.BlockSpec((len(POOL_WINDOWS), POOL_GROUP, POOL_GROUP), lambda b, i: (0, 0, 0)),
                  pl.BlockSpec((1, POOL_WIDTH), fix2)],
        out_specs=[out_spec, out_spec],
        out_shape=[jax.ShapeDtypeStruct((bsz, L, CF_WIDTH), BF16)] * 2,
        scratch_shapes=[pltpu.VMEM((tl + 2 * HALO, CF_WIDTH), F32),
                        pltpu.VMEM((tl + 2 * HALO, POOL_WIDTH), F32),
                        pltpu.VMEM((tl, CF_WIDTH), F32),
                        pltpu.VMEM((SUBLANES, tl + 2 * HALO, CF_WIDTH), F32)],
        compiler_params=_params("parallel", "parallel"),
        name="odd_mix",
    )(x, x, x, w_in, b_in, dw_w, dw_b, ln_g, ln_b, pool_w, pool_scale)


def _rope_tables(L):
    inv = 1.0 / (ROPE_THETA ** (jnp.arange(0, MLA_ROPE, 2, dtype=F32) / MLA_ROPE))
    ang = jnp.arange(L, dtype=F32)[:, None] * inv[None, :]
    cos, sin = jnp.cos(ang), jnp.sin(ang)
    scale = (MLA_NOPE + MLA_ROPE) ** -0.5 * math.log2(math.e)
    ones = jnp.ones((L, MLA_NOPE), F32)
    zeros = jnp.zeros((L, MLA_NOPE), F32)
    pad = jnp.zeros((L, HEAD_PAD - MLA_NOPE - MLA_ROPE), F32)
    qc = scale * jnp.concatenate([ones, cos, cos, pad], axis=1)
    qs = scale * jnp.concatenate([zeros, sin, sin, pad], axis=1)
    kt = jnp.concatenate([cos, cos, sin, sin, jnp.zeros((L, LANES - 2 * MLA_ROPE), F32)], axis=1)
    return qc, qs, kt


def _rot_half_cols(w):
    half = w.shape[-1] // 2
    return jnp.concatenate([-w[..., half:], w[..., :half]], axis=-1)


def _prep_even(ev_w_in, mla_w_uq, mla_w_ukv):
    k = ev_w_in.shape[0]
    base = 3 * HY_WIDTH + MLA_Q_RANK + MLA_KV_RANK
    kr = ev_w_in[:, base:base + MLA_ROPE]
    w_in = jnp.concatenate(
        [ev_w_in[:, :base], kr, _rot_half_cols(kr),
         jnp.zeros((k, LANES - 2 * MLA_ROPE), F32)], axis=1).astype(BF16)
    uq = mla_w_uq.reshape(MLA_Q_RANK, MLA_HEADS, MLA_NOPE + MLA_ROPE)
    zq = jnp.zeros((MLA_Q_RANK, MLA_HEADS, HEAD_PAD - MLA_NOPE - MLA_ROPE), F32)
    wqa = jnp.concatenate([uq, zq], axis=2).reshape(MLA_Q_RANK, -1).astype(BF16)
    wqb = jnp.concatenate([jnp.zeros_like(uq[:, :, :MLA_NOPE]), _rot_half_cols(uq[:, :, MLA_NOPE:]), zq],
                          axis=2).reshape(MLA_Q_RANK, -1).astype(BF16)
    ukv = mla_w_ukv.reshape(MLA_KV_RANK, MLA_HEADS, MLA_NOPE + MLA_V)
    zk = jnp.zeros((MLA_KV_RANK, MLA_HEADS, HEAD_PAD - MLA_NOPE), F32)
    wk = jnp.concatenate([ukv[:, :, :MLA_NOPE], zk], axis=2).reshape(MLA_KV_RANK, -1).astype(BF16)
    zv = jnp.zeros((MLA_KV_RANK, MLA_HEADS, VT_ROWS - MLA_V), F32)
    wvt = jnp.concatenate([ukv[:, :, MLA_NOPE:], zv], axis=2).reshape(MLA_KV_RANK, -1).T.astype(BF16)
    return w_in, wqa, wqb, wk, wvt


def _rope_place():
    e = np.zeros((LANES, MLA_HEADS * HEAD_PAD), np.float32)
    for h in range(MLA_HEADS):
        for j in range(MLA_ROPE):
            e[j, h * HEAD_PAD + MLA_NOPE + j] = 1.0
    return jnp.asarray(e, BF16)


def _trunk(x, w, filt):
    bsz, L, d = x.shape
    t = bsz * L
    xf = x.reshape(t, d)
    qc, qs, kt = _rope_tables(L)
    place = _rope_place()
    zero_b = jnp.zeros((1, d), F32)
    for layer in range(DEPTH):
        i = layer // 2
        if layer % 2 == 0:
            w_in, wqa, wqb, wk, wvt = w["even"][i]
            z, x0c, p_mla = _even_in(xf.reshape(bsz, L, d), w_in, w["hy_conv_w"][i],
                                     w["hy_conv_b"][i].reshape(1, -1))
            gr, gi = filt[(L, i)]
            m1 = _hy_conv(z, x0c, gr, gi, w["hy_skip"][i], L).reshape(t, HY_WIDTH)
            q, k, vt = _mla_proj(p_mla.reshape(t, -1), L, w["mla_q_norm"][i].reshape(1, -1), wqa, wqb,
                                 w["mla_kv_norm"][i].reshape(1, -1), wk, wvt, place, qc, qs, kt)
            hw = MLA_HEADS * HEAD_PAD
            m2 = _attention(q.reshape(bsz, L, hw), k.reshape(bsz, L, hw), vt).reshape(t, MLA_HEADS * MLA_V)
            w_out = w["ev_w_out"][i]
            b_out = zero_b
        else:
            cf, pool = _odd_mix(xf.reshape(bsz, L, d), w["od_w_in"][i], w["od_b_in"][i].reshape(1, -1),
                                w["cf_dw_w"][i], w["cf_dw_b"][i].reshape(1, -1),
                                w["cf_ln_g"][i].reshape(1, -1), w["cf_ln_b"][i].reshape(1, -1),
                                w["pool_w"][i], w["pool_scale"][i].reshape(1, -1))
            m1 = cf.reshape(t, CF_WIDTH)
            m2 = pool.reshape(t, POOL_WIDTH)
            w_out = w["od_w_out"][i]
            b_out = w["od_b_out"][i].reshape(1, -1)
        half = w_out.shape[0] // 2
        xf = _layer_tail(m1, m2, xf, w_out[:half], w_out[half:], b_out,
                         w["ln1_g"][layer].reshape(1, -1), w["ln1_b"][layer].reshape(1, -1),
                         w["mlp_w1"][layer], w["mlp_w2"][layer],
                         w["ln2_g"][layer].reshape(1, -1), w["ln2_b"][layer].reshape(1, -1))
    return xf.reshape(bsz, L, d)


def kernel(x_prompt, x_sample, ev_w_in, hy_conv_w, hy_conv_b, hy_filt_w1, hy_filt_b1, hy_filt_w_inner, hy_filt_b_inner, hy_filt_freq, hy_filt_w_out, hy_skip, mla_q_norm, mla_w_uq, mla_kv_norm, mla_w_ukv, ev_w_out, od_w_in, od_b_in, cf_dw_w, cf_dw_b, cf_ln_g, cf_ln_b, pool_w, pool_scale, od_w_out, od_b_out, ln1_g, ln1_b, mlp_w1, mlp_w2, ln2_g, ln2_b):
    n_even = ev_w_in.shape[0]
    w = {
        "even": [_prep_even(ev_w_in[i], mla_w_uq[i], mla_w_ukv[i]) for i in range(n_even)],
        "hy_conv_w": hy_conv_w, "hy_conv_b": hy_conv_b, "hy_skip": hy_skip,
        "mla_q_norm": mla_q_norm, "mla_kv_norm": mla_kv_norm,
        "ev_w_out": ev_w_out.astype(BF16),
        "od_w_in": od_w_in.astype(BF16), "od_b_in": od_b_in,
        "cf_dw_w": cf_dw_w, "cf_dw_b": cf_dw_b, "cf_ln_g": cf_ln_g, "cf_ln_b": cf_ln_b,
        "pool_w": pool_w.astype(BF16), "pool_scale": pool_scale,
        "od_w_out": od_w_out.astype(BF16), "od_b_out": od_b_out,
        "ln1_g": ln1_g, "ln1_b": ln1_b, "ln2_g": ln2_g, "ln2_b": ln2_b,
        "mlp_w1": mlp_w1.astype(BF16), "mlp_w2": mlp_w2.astype(BF16),
    }
    filt = {}
    for L in sorted({x_prompt.shape[1], x_sample.shape[1]}):
        for i in range(n_even):
            g = _hy_filter(L, hy_filt_w1[i], hy_filt_b1[i], hy_filt_w_inner[i], hy_filt_b_inner[i],
                           hy_filt_freq[i], hy_filt_w_out[i])
            filt[(L, i)] = _hy_spectrum(g, L)
    return (_trunk(x_prompt, w, filt), _trunk(x_sample, w, filt))
```

```python
import functools
import math

import numpy as np
import jax
import jax.numpy as jnp
from jax import lax
from jax.experimental import pallas as pl
from jax.experimental.pallas import tpu as pltpu

F32 = jnp.float32
BF16 = jnp.bfloat16

D_MODEL = 1024
DEPTH = 4
HY_WIDTH = D_MODEL // 2
HY_EMB = 33
HY_BANDS = (HY_EMB - 1) // 2
HY_FILT = 64
HY_INNER = 2
HY_MAX_DECAY = math.log(1e-2) / 0.3
HY_MIN_DECAY = math.log(1e-2) / 1.5
MLA_HEADS = 8
MLA_NOPE = 64
MLA_ROPE = 32
MLA_V = 64
MLA_Q_RANK = 384
MLA_KV_RANK = 256
ROPE_THETA = 10000.0
CF_WIDTH = D_MODEL // 2
CF_KERNEL = 31
POOL_WIDTH = D_MODEL // 2
POOL_WINDOWS = (2, 4, 8, 16)
POOL_GROUP = POOL_WIDTH // len(POOL_WINDOWS)
D_FF = 4 * D_MODEL
DN_ALPHA = (2 * DEPTH) ** 0.25
LN_EPS = 1e-5
RMS_EPS = 1e-6

LANES = 128
SUBLANES = 8
VMEM_LIMIT = 52 * 1024 * 1024
HEAD_PAD = 128
VT_ROWS = 80
EVEN_COLS = 3 * HY_WIDTH + 6 * LANES
FFT_N2 = 32
FFT_PITCH = FFT_N2 + SUBLANES
FFT_MAX_UNROLL = 18
HALO = 16


def _params(*sem):
    return pltpu.CompilerParams(dimension_semantics=sem, vmem_limit_bytes=VMEM_LIMIT)


def _layer_norm(r, g, b):
    mu = jnp.mean(r, axis=-1, keepdims=True)
    d = r - mu
    var = jnp.mean(d * d, axis=-1, keepdims=True)
    return d * lax.rsqrt(var + LN_EPS) * g + b


def _bdot(a, b):
    return jnp.dot(a.astype(BF16), b.astype(BF16), preferred_element_type=F32)


def _layer_tail_body(m1_ref, m2_ref, x_ref, wo1_ref, wo2_ref, bo_ref, g1_ref, b1_ref,
                     w1_ref, w2_ref, g2_ref, b2_ref, o_ref, *, ff_chunk, streams):
    rows = x_ref.shape[0] // streams
    sls = [slice(s * rows, (s + 1) * rows) for s in range(streams)]
    x1 = []
    for sl in sls:
        y = (jnp.dot(m1_ref[sl, :], wo1_ref[...], preferred_element_type=F32)
             + jnp.dot(m2_ref[sl, :], wo2_ref[...], preferred_element_type=F32) + bo_ref[...])
        x1.append(_layer_norm(DN_ALPHA * x_ref[sl, :] + y, g1_ref[...], b1_ref[...]))
    for sl, xs in zip(sls, x1):
        xb = xs.astype(BF16)
        acc = DN_ALPHA * xs
        for c in range(D_FF // ff_chunk):
            h = jnp.dot(xb, w1_ref[:, c * ff_chunk:(c + 1) * ff_chunk], preferred_element_type=F32)
            h = jnp.maximum(h, 0.0)
            acc = acc + jnp.dot((h * h).astype(BF16), w2_ref[c * ff_chunk:(c + 1) * ff_chunk, :],
                                preferred_element_type=F32)
        o_ref[sl, :] = _layer_norm(acc, g2_ref[...], b2_ref[...])


def _layer_tail(m1, m2, x, wo1, wo2, bo, g1, b1, w1, w2, g2, b2, tm=512, ff_chunk=1024, streams=2):
    t, d = x.shape
    k = m1.shape[1]
    row = lambda i: (i, 0)
    fix = lambda i: (0, 0)
    resident = lambda shape: pl.BlockSpec(shape, fix, pipeline_mode=pl.Buffered(1))
    vec = pl.BlockSpec((1, d), fix)
    return pl.pallas_call(
        functools.partial(_layer_tail_body, ff_chunk=ff_chunk, streams=streams),
        grid=(t // tm,),
        in_specs=[pl.BlockSpec((tm, k), row), pl.BlockSpec((tm, k), row), pl.BlockSpec((tm, d), row),
                  resident((k, d)), resident((k, d)), vec, vec, vec,
                  resident((d, D_FF)), resident((D_FF, d)), vec, vec],
        out_specs=pl.BlockSpec((tm, d), row),
        out_shape=jax.ShapeDtypeStruct((t, d), F32),
        compiler_params=_params("parallel"),
        name="layer_tail",
    )(m1, m2, x, wo1, wo2, bo, g1, b1, w1, w2, g2, b2)


def _halo_specs(tl, L):
    per = tl // HALO
    nblk = L // HALO

    def main(b, i):
        return (b, i, 0)

    def left(b, i):
        return (b, jnp.maximum(i * per - 1, 0), 0)

    def right(b, i):
        return (b, jnp.minimum((i + 1) * per, nblk - 1), 0)

    return main, left, right


def _even_in_body(xm_ref, xl_ref, xr_ref, w_ref, cw_ref, cb_ref, z_ref, x0_ref, pm_ref, ext_ref, *, tl):
    i = pl.program_id(1)
    last = pl.num_programs(1) - 1
    hw = 3 * HY_WIDTH
    xm = xm_ref[0].astype(BF16)
    xl = xl_ref[0].astype(BF16)
    xr = xr_ref[0].astype(BF16)

    def project(part):
        cols = slice(part * HY_WIDTH, (part + 1) * HY_WIDTH)
        ext_ref[0:HALO, cols] = jnp.where(i == 0, 0.0, _bdot(xl, w_ref[:, cols]))
        ext_ref[HALO:HALO + tl, cols] = _bdot(xm, w_ref[:, cols])
        ext_ref[HALO + tl:, cols] = jnp.where(i == last, 0.0, _bdot(xr, w_ref[:, cols]))

    def conv(part):
        cols = slice(part * HY_WIDTH, (part + 1) * HY_WIDTH)
        return (cw_ref[0:1, cols] * ext_ref[HALO - 1:HALO - 1 + tl, cols]
                + cw_ref[1:2, cols] * ext_ref[HALO:HALO + tl, cols]
                + cw_ref[2:3, cols] * ext_ref[HALO + 1:HALO + 1 + tl, cols] + cb_ref[:, cols])

    project(0)
    project(1)
    x0_ref[0] = conv(0)
    project(2)
    x1c = conv(1)
    pm_ref[0] = _bdot(xm, w_ref[:, hw:])
    z_ref[0] = x1c * conv(2)


def _even_in(x, w_in, conv_w, conv_b, tl=512):
    bsz, L, d = x.shape
    hw = 3 * HY_WIDTH
    rest = EVEN_COLS - hw
    main, left, right = _halo_specs(tl, L)
    fix = lambda b, i: (0, 0)
    hy_spec = pl.BlockSpec((1, tl, HY_WIDTH), main)
    return pl.pallas_call(
        functools.partial(_even_in_body, tl=tl),
        grid=(bsz, L // tl),
        in_specs=[pl.BlockSpec((1, tl, d), main), pl.BlockSpec((1, HALO, d), left),
                  pl.BlockSpec((1, HALO, d), right),
                  pl.BlockSpec((d, EVEN_COLS), fix), pl.BlockSpec((3, hw), fix), pl.BlockSpec((1, hw), fix)],
        out_specs=[hy_spec, hy_spec, pl.BlockSpec((1, tl, rest), main)],
        out_shape=[jax.ShapeDtypeStruct((bsz, L, HY_WIDTH), F32)] * 2
        + [jax.ShapeDtypeStruct((bsz, L, rest), F32)],
        scratch_shapes=[pltpu.VMEM((tl + 2 * HALO, hw), F32)],
        compiler_params=_params("parallel", "parallel"),
        name="even_in",
    )(x, x, x, w_in, conv_w, conv_b)


def _hy_filter_body(wt_ref, wc_ref, ws_ref, b1_ref, wi_ref, bi_ref, fr_ref, wo_ref, g_ref, *, L, tr):
    hi = lax.Precision.HIGHEST
    dot = functools.partial(jnp.dot, precision=hi, preferred_element_type=F32)
    scale_t = 1.0 / (L - 1)
    m_row = pl.program_id(0) * tr + lax.broadcasted_iota(jnp.int32, (1, tr), 1)
    j_row = jnp.where(m_row < L, m_row, 2 * L - m_row).astype(F32)
    band_step = (HY_BANDS - 1 - 1e-4) / (HY_BANDS - 1)
    bands = 1e-4 + band_step * lax.broadcasted_iota(jnp.int32, (HY_BANDS, 1), 0).astype(F32)
    arg = bands * (j_row * (2.0 * math.pi / L))
    freq = fr_ref[...]
    pre = (wt_ref[...] * (j_row * scale_t) + dot(wc_ref[...], jnp.cos(arg))
           - dot(ws_ref[...], jnp.sin(arg)) + b1_ref[...])
    h = jnp.sin(freq * pre)
    for l in range(HY_INNER):
        h = jnp.sin(freq * (dot(wi_ref[l], h) + bi_ref[l]))
    k = dot(h.T, wo_ref[...])
    m = pl.program_id(0) * tr + lax.broadcasted_iota(jnp.int32, (tr, 1), 0)
    t = jnp.where(m < L, m, 2 * L - m).astype(F32) * scale_t
    ch = lax.broadcasted_iota(jnp.int32, (1, HY_WIDTH), 1).astype(F32)
    deltas = jnp.abs(HY_MIN_DECAY + ch * ((HY_MAX_DECAY - HY_MIN_DECAY) / (HY_WIDTH - 1)))
    g_ref[...] = jnp.where(m == L, 0.0, k * jnp.exp(-t * deltas))


def _hy_filter(L, w1, b1, w_inner, b_inner, freq, w_out, tr=512):
    n = 2 * L
    full = lambda *shape: pl.BlockSpec(shape, lambda i: (0,) * len(shape))
    col = lambda a: a.reshape(-1, 1)
    w1t = w1.T
    return pl.pallas_call(
        functools.partial(_hy_filter_body, L=L, tr=tr),
        grid=(n // tr,),
        in_specs=[full(HY_FILT, 1), full(HY_FILT, HY_BANDS), full(HY_FILT, HY_BANDS), full(HY_FILT, 1),
                  full(HY_INNER, HY_FILT, HY_FILT), full(HY_INNER, HY_FILT, 1), full(HY_FILT, 1),
                  pl.BlockSpec((HY_FILT, HY_WIDTH), lambda i: (0, i // (L // tr)))],
        out_specs=pl.BlockSpec((tr, HY_WIDTH), lambda i: (i, 0)),
        out_shape=jax.ShapeDtypeStruct((n, HY_WIDTH), F32),
        compiler_params=_params("parallel"),
        name="hy_filter",
    )(w1t[:, 0:1], w1t[:, 1:1 + HY_BANDS], w1t[:, 1 + HY_BANDS:], col(b1),
      jnp.swapaxes(w_inner, 1, 2), b_inner.reshape(HY_INNER, HY_FILT, 1), col(freq), w_out)


def _fft_dims(L):
    n = 2 * L
    n1 = n // FFT_N2
    return n, n1, n1 // 2 + SUBLANES, n1 // 2


@functools.lru_cache(maxsize=None)
def _fft_tables(L):
    n, n1, kh, h1 = _fft_dims(L)
    n2 = FFT_N2
    k1 = np.arange(kh, dtype=np.float64)[:, None]
    t1 = np.arange(n1, dtype=np.float64)[None, :]
    th = 2.0 * np.pi * k1 * t1 / n1
    fy = np.concatenate([np.cos(th), -np.sin(th)], axis=0)
    k2 = np.arange(n2, dtype=np.float64)[None, :, None]
    t2 = np.arange(n2, dtype=np.float64)[None, None, :]
    ph = 2.0 * np.pi * (k2 * t2 / n2 + k1[:, :, None] * t2 / n)
    mr, mi = np.cos(ph), -np.sin(ph)
    mx = np.concatenate([np.concatenate([mr, -mi], axis=2),
                         np.concatenate([mi, mr], axis=2)], axis=1)
    mxi = np.transpose(mx, (0, 2, 1))
    wk = np.zeros((kh,), np.float64)
    wk[:n1 // 2 + 1] = 2.0
    wk[0] = 1.0
    wk[n1 // 2] = 1.0
    thi = th[:, :h1].T
    fyi = np.concatenate([np.cos(thi) * wk[None, :], -np.sin(thi) * wk[None, :]], axis=1)
    return fy, mx, mxi, fyi


def _fft_unroll(kh):
    return max(u for u in range(1, FFT_MAX_UNROLL + 1) if kh % u == 0)


def _pitch_rows(src_ref, dst_ref, nblk, batch_idx=None):
    def body(t1, carry):
        s = pl.multiple_of(t1 * FFT_N2, SUBLANES)
        d = pl.multiple_of(t1 * FFT_PITCH, SUBLANES)
        if batch_idx is None:
            dst_ref[pl.ds(d, FFT_N2), :] = src_ref[pl.ds(s, FFT_N2), :]
        else:
            dst_ref[pl.ds(d, FFT_N2), :] = src_ref[batch_idx, pl.ds(s, FFT_N2), :]
        return carry
    lax.fori_loop(0, nblk, body, 0)


def _fft_stage_y(src_ref, fy_ref, ar_ref, ai_ref, nblk, kh):
    fy = fy_ref[...]
    for t2 in range(FFT_N2):
        x = src_ref[pl.ds(t2, nblk, stride=FFT_PITCH), :].astype(BF16)
        a = jnp.dot(fy, x, preferred_element_type=F32)
        ar_ref[pl.ds(t2, kh, stride=FFT_PITCH), :] = a[:kh]
        ai_ref[pl.ds(t2, kh, stride=FFT_PITCH), :] = a[kh:]


def _hy_spectrum_body(g_ref, fy_ref, mx_ref, gr_ref, gi_ref, gs_ref, ar_ref, ai_ref, *, L):
    n, n1, kh, _ = _fft_dims(L)
    _pitch_rows(g_ref, gs_ref, n1)
    _fft_stage_y(gs_ref, fy_ref, ar_ref, ai_ref, n1, kh)
    unroll = _fft_unroll(kh)

    def body(it, carry):
        k1s = [it * unroll + u for u in range(unroll)]
        a = []
        for k1 in k1s:
            src = pl.multiple_of(k1 * FFT_PITCH, SUBLANES)
            a.append(jnp.concatenate([ar_ref[pl.ds(src, FFT_N2), :], ai_ref[pl.ds(src, FFT_N2), :]],
                                     axis=0).astype(BF16))
        z = [jnp.dot(mx_ref[k1], x, preferred_element_type=F32) * (1.0 / n) for k1, x in zip(k1s, a)]
        for k1, zz in zip(k1s, z):
            dst = pl.multiple_of(k1 * FFT_N2, SUBLANES)
            gr_ref[pl.ds(dst, FFT_N2), :] = zz[:FFT_N2]
            gi_ref[pl.ds(dst, FFT_N2), :] = zz[FFT_N2:]
        return carry
    lax.fori_loop(0, kh // unroll, body, 0)


def _hy_spectrum(g, L):
    n, n1, kh, _ = _fft_dims(L)
    fy, mx, _, _ = _fft_tables(L)
    out_spec = pl.BlockSpec((kh * FFT_N2, LANES), lambda c: (0, c))
    return pl.pallas_call(
        functools.partial(_hy_spectrum_body, L=L),
        grid=(HY_WIDTH // LANES,),
        in_specs=[pl.BlockSpec((n, LANES), lambda c: (0, c)),
                  pl.BlockSpec((2 * kh, n1), lambda c: (0, 0)),
                  pl.BlockSpec((kh, 2 * FFT_N2, 2 * FFT_N2), lambda c: (0, 0, 0))],
        out_specs=[out_spec, out_spec],
        out_shape=[jax.ShapeDtypeStruct((kh * FFT_N2, HY_WIDTH), F32)] * 2,
        scratch_shapes=[pltpu.VMEM((n1 * FFT_PITCH, LANES), F32),
                        pltpu.VMEM((kh * FFT_PITCH, LANES), F32),
                        pltpu.VMEM((kh * FFT_PITCH, LANES), F32)],
        compiler_params=_params("parallel"),
        name="hy_spectrum",
    )(g, jnp.asarray(fy, BF16), jnp.asarray(mx, BF16))


def _hy_conv_body(z_ref, x0_ref, gr_ref, gi_ref, fy_ref, mx_ref, mxi_ref, fyi_ref, skip_ref, o_ref,
                  zs_ref, ar_ref, ai_ref, *, L):
    n, n1, kh, h1 = _fft_dims(L)
    _pitch_rows(z_ref, zs_ref, h1, batch_idx=0)
    _fft_stage_y(zs_ref, fy_ref, ar_ref, ai_ref, h1, kh)

    unroll = _fft_unroll(kh)

    def spectral(it, carry):
        k1s = [it * unroll + u for u in range(unroll)]
        rows = [pl.multiple_of(k1 * FFT_PITCH, SUBLANES) for k1 in k1s]
        grows = [pl.multiple_of(k1 * FFT_N2, SUBLANES) for k1 in k1s]
        a = [jnp.concatenate([ar_ref[pl.ds(r, FFT_N2), :], ai_ref[pl.ds(r, FFT_N2), :]], axis=0).astype(BF16)
             for r in rows]
        z = [jnp.dot(mx_ref[k1], x, preferred_element_type=F32) for k1, x in zip(k1s, a)]
        y = []
        for zz, g in zip(z, grows):
            zr, zi = zz[:FFT_N2], zz[FFT_N2:]
            gr = gr_ref[pl.ds(g, FFT_N2), :]
            gi = gi_ref[pl.ds(g, FFT_N2), :]
            y.append(jnp.concatenate([zr * gr - zi * gi, zr * gi + zi * gr], axis=0).astype(BF16))
        b = [jnp.dot(mxi_ref[k1], x, preferred_element_type=F32) for k1, x in zip(k1s, y)]
        for r, bb in zip(rows, b):
            ar_ref[pl.ds(r, FFT_N2), :] = bb[:FFT_N2]
            ai_ref[pl.ds(r, FFT_N2), :] = bb[FFT_N2:]
        return carry
    lax.fori_loop(0, kh // unroll, spectral, 0)

    fyi = fyi_ref[...]
    for t2 in range(FFT_N2):
        b = jnp.concatenate([ar_ref[pl.ds(t2, kh, stride=FFT_PITCH), :],
                             ai_ref[pl.ds(t2, kh, stride=FFT_PITCH), :]], axis=0)
        zs_ref[pl.ds(t2, h1, stride=FFT_PITCH), :] = jnp.dot(fyi, b.astype(BF16), preferred_element_type=F32)

    skip = skip_ref[...]

    def finish(t1, carry):
        s = pl.multiple_of(t1 * FFT_N2, SUBLANES)
        d = pl.multiple_of(t1 * FFT_PITCH, SUBLANES)
        y = zs_ref[pl.ds(d, FFT_N2), :] + z_ref[0, pl.ds(s, FFT_N2), :] * skip
        o_ref[0, pl.ds(s, FFT_N2), :] = (x0_ref[0, pl.ds(s, FFT_N2), :] * y).astype(o_ref.dtype)
        return carry
    lax.fori_loop(0, h1, finish, 0)


def _hy_conv(z, x0c, gr, gi, skip, L):
    bsz = z.shape[0]
    n, n1, kh, h1 = _fft_dims(L)
    fy, mx, mxi, fyi = _fft_tables(L)
    seq = pl.BlockSpec((1, L, LANES), lambda c, b: (b, 0, c))
    spec = pl.BlockSpec((kh * FFT_N2, LANES), lambda c, b: (0, c))
    mat = pl.BlockSpec((kh, 2 * FFT_N2, 2 * FFT_N2), lambda c, b: (0, 0, 0))
    return pl.pallas_call(
        functools.partial(_hy_conv_body, L=L),
        grid=(HY_WIDTH // LANES, bsz),
        in_specs=[seq, seq, spec, spec,
                  pl.BlockSpec((2 * kh, h1), lambda c, b: (0, 0)),
                  mat, mat,
                  pl.BlockSpec((h1, 2 * kh), lambda c, b: (0, 0)),
                  pl.BlockSpec((1, LANES), lambda c, b: (0, c))],
        out_specs=seq,
        out_shape=jax.ShapeDtypeStruct((bsz, L, HY_WIDTH), BF16),
        scratch_shapes=[pltpu.VMEM((h1 * FFT_PITCH, LANES), F32),
                        pltpu.VMEM((kh * FFT_PITCH, LANES), F32),
                        pltpu.VMEM((kh * FFT_PITCH, LANES), F32)],
        compiler_params=_params("parallel", "parallel"),
        name="hy_conv",
    )(z, x0c, gr, gi, jnp.asarray(fy[:, :h1], BF16), jnp.asarray(mx, BF16), jnp.asarray(mxi, BF16),
      jnp.asarray(fyi, BF16), skip.reshape(1, -1))


def _rms(x, g):
    return x * lax.rsqrt(jnp.mean(x * x, axis=-1, keepdims=True) + RMS_EPS) * g


def _mla_proj_body(p_ref, qn_ref, wq_ref, kvn_ref, wk_ref, wvt_ref,
                   qc_ref, qlo_ref, qhi_ref, kt_ref, q_ref, k_ref, vt_ref):
    p = p_ref[...]
    cq = _rms(p[:, :MLA_Q_RANK], qn_ref[...]).astype(BF16)
    ckv = _rms(p[:, MLA_Q_RANK:MLA_Q_RANK + MLA_KV_RANK], kvn_ref[...]).astype(BF16)
    q = jnp.dot(cq, wq_ref[...], preferred_element_type=F32)
    k = jnp.dot(ckv, wk_ref[...], preferred_element_type=F32)
    kr = p[:, MLA_Q_RANK + MLA_KV_RANK:] * kt_ref[...]
    kr = kr + pltpu.roll(kr, LANES - MLA_ROPE, axis=1)
    lane = lax.broadcasted_iota(jnp.int32, (1, HEAD_PAD), 1)
    in_rope = (lane >= MLA_NOPE) & (lane < MLA_NOPE + MLA_ROPE)
    kr = jnp.where(in_rope, pltpu.roll(kr, MLA_NOPE, axis=1), 0.0)
    qc = qc_ref[...]
    qlo = qlo_ref[...]
    qhi = qhi_ref[...]
    half = MLA_ROPE // 2
    for h in range(MLA_HEADS):
        sl = slice(h * HEAD_PAD, (h + 1) * HEAD_PAD)
        x = q[:, sl]
        q_ref[:, sl] = (x * qc + pltpu.roll(x, HEAD_PAD - half, axis=1) * qlo
                        + pltpu.roll(x, half, axis=1) * qhi).astype(BF16)
        k_ref[:, sl] = (k[:, sl] + kr).astype(BF16)
    vt = lax.dot_general(wvt_ref[...], ckv, (((1,), (1,)), ((), ())), preferred_element_type=F32)
    row = lax.broadcasted_iota(jnp.int32, (MLA_HEADS * VT_ROWS, 1), 0) % VT_ROWS
    vt_ref[0] = (vt + jnp.where(row == MLA_V, 1.0, 0.0)).astype(BF16)


def _mla_proj(p, L, q_norm, wq, kv_norm, wk, wvt, qc, qlo, qhi, kt, tm=512):
    t = p.shape[0]
    hw = MLA_HEADS * HEAD_PAD
    nper = L // tm
    fix = lambda i: (0, 0)
    pos = lambda i: (i % nper, 0)
    row = lambda i: (i, 0)
    wide = 6 * LANES
    return pl.pallas_call(
        _mla_proj_body,
        grid=(t // tm,),
        in_specs=[pl.BlockSpec((tm, wide), row),
                  pl.BlockSpec((1, MLA_Q_RANK), fix),
                  pl.BlockSpec((MLA_Q_RANK, hw), fix),
                  pl.BlockSpec((1, MLA_KV_RANK), fix),
                  pl.BlockSpec((MLA_KV_RANK, hw), fix),
                  pl.BlockSpec((MLA_HEADS * VT_ROWS, MLA_KV_RANK), fix),
                  pl.BlockSpec((tm, HEAD_PAD), pos), pl.BlockSpec((tm, HEAD_PAD), pos),
                  pl.BlockSpec((tm, HEAD_PAD), pos), pl.BlockSpec((tm, LANES), pos)],
        out_specs=[pl.BlockSpec((tm, hw), row), pl.BlockSpec((tm, hw), row),
                   pl.BlockSpec((1, MLA_HEADS * VT_ROWS, tm), lambda i: (i // nper, 0, i % nper))],
        out_shape=[jax.ShapeDtypeStruct((t, hw), BF16), jax.ShapeDtypeStruct((t, hw), BF16),
                   jax.ShapeDtypeStruct((t // L, MLA_HEADS * VT_ROWS, L), BF16)],
        compiler_params=_params("parallel"),
        name="mla_proj",
    )(p, q_norm, wq, kv_norm, wk, wvt, qc, qlo, qhi, kt)


def _attn_body(q_ref, k_ref, vt_ref, o_ref, *, kc):
    L = k_ref.shape[1]
    nchunk = L // kc
    heads = range(2)
    qs = [q_ref[0, :, h * HEAD_PAD:(h + 1) * HEAD_PAD] for h in heads]

    def scores(h, c):
        return lax.dot_general(k_ref[0, c * kc:(c + 1) * kc, h * HEAD_PAD:(h + 1) * HEAD_PAD], qs[h],
                               (((1,), (1,)), ((), ())), preferred_element_type=F32)

    m = [None, None]
    acc = [None, None]
    s_next = [scores(h, 0) for h in heads]
    for c in range(nchunk):
        s_cur = s_next
        if c + 1 < nchunk:
            s_next = [scores(h, c + 1) for h in heads]
        for h in heads:
            s = s_cur[h]
            cmax = jnp.max(s, axis=0, keepdims=True)
            m_new = cmax if c == 0 else jnp.maximum(m[h], cmax)
            e = jnp.exp2(s - m_new).astype(BF16)
            pv = jnp.dot(vt_ref[0, h * VT_ROWS:(h + 1) * VT_ROWS, c * kc:(c + 1) * kc], e,
                         preferred_element_type=F32)
            acc[h] = pv if c == 0 else acc[h] * jnp.exp2(m[h] - m_new) + pv
            m[h] = m_new
    out_t = jnp.concatenate([acc[h][:MLA_V] / acc[h][MLA_V:MLA_V + 1] for h in heads], axis=0)
    o_ref[0] = out_t.T.astype(o_ref.dtype)


def _attention(q, k, vt, tq=1024, kc=256):
    bsz, L, _ = q.shape
    return pl.pallas_call(
        functools.partial(_attn_body, kc=kc),
        grid=(bsz, MLA_HEADS // 2, L // tq),
        in_specs=[pl.BlockSpec((1, tq, 2 * HEAD_PAD), lambda b, g, i: (b, i, g)),
                  pl.BlockSpec((1, L, 2 * HEAD_PAD), lambda b, g, i: (b, 0, g)),
                  pl.BlockSpec((1, 2 * VT_ROWS, L), lambda b, g, i: (b, g, 0))],
        out_specs=pl.BlockSpec((1, tq, 2 * MLA_V), lambda b, g, i: (b, i, g)),
        out_shape=jax.ShapeDtypeStruct((bsz, L, MLA_HEADS * MLA_V), BF16),
        compiler_params=_params("parallel", "parallel", "parallel"),
        name="attention",
    )(q, k, vt)


def _linear_body(x_ref, w_ref, b_ref, o_ref):
    o_ref[...] = _bdot(x_ref[...], w_ref[...]) + b_ref[...]


def _linear(x, w, b, tm=512):
    t, k = x.shape
    n = w.shape[1]
    return pl.pallas_call(
        _linear_body,
        grid=(t // tm,),
        in_specs=[pl.BlockSpec((tm, k), lambda i: (i, 0)),
                  pl.BlockSpec((k, n), lambda i: (0, 0)),
                  pl.BlockSpec((1, n), lambda i: (0, 0))],
        out_specs=pl.BlockSpec((tm, n), lambda i: (i, 0)),
        out_shape=jax.ShapeDtypeStruct((t, n), F32),
        compiler_params=_params("parallel"),
        name="linear",
    )(x, w, b)


def _odd_mix_body(um_ref, ul_ref, ur_ref, dw_ref, db_ref, lg_ref, lb_ref, pw_ref, ps_ref,
                  cf_ref, pool_ref, h_ref, u_ref, conv_ref, hp_ref, sa_ref, sb_ref, *, tl, L, rows):
    i = pl.program_id(1)
    last = pl.num_programs(1) - 1

    def glu(u):
        return u[:, :CF_WIDTH] * jax.nn.sigmoid(u[:, CF_WIDTH:2 * CF_WIDTH])

    ul = ul_ref[0]
    um = um_ref[0]
    ur = ur_ref[0]
    h_ref[0:HALO, :] = jnp.where(i == 0, 0.0, glu(ul))
    h_ref[HALO:HALO + tl, :] = glu(um)
    h_ref[HALO + tl:, :] = jnp.where(i == last, 0.0, glu(ur))
    u_ref[0:HALO, :] = jnp.where(i == 0, 0.0, ul[:, 2 * CF_WIDTH:])
    u_ref[HALO:HALO + tl, :] = um[:, 2 * CF_WIDTH:]
    u_ref[HALO + tl:, :] = jnp.where(i == last, 0.0, ur[:, 2 * CF_WIDTH:])

    span = tl + 2 * HALO - SUBLANES
    for j in range(SUBLANES):
        hp_ref[j, 0:span, :] = h_ref[j:j + span, :]

    off = HALO - CF_KERNEL // 2
    for r in range(0, tl, rows):
        for c in range(0, CF_WIDTH, LANES):
            acc = jnp.zeros((rows, LANES), F32) + db_ref[:, c:c + LANES]
            for k in range(CF_KERNEL):
                j = (off + k) % SUBLANES
                base = r + off + k - j
                acc = acc + dw_ref[k:k + 1, c:c + LANES] * hp_ref[j, base:base + rows, c:c + LANES]
            conv_ref[r:r + rows, c:c + LANES] = acc
    hn = _layer_norm(conv_ref[...], lg_ref[...], lb_ref[...])
    cf_ref[0] = (hn * jax.nn.sigmoid(hn)).astype(cf_ref.dtype)

    pos = i * tl + lax.broadcasted_iota(jnp.int32, (tl, 1), 0)
    ext = tl + 2 * HALO
    first, stop = SUBLANES, ext - SUBLANES
    for buf in (sa_ref, sb_ref):
        buf[stop:ext, :] = jnp.zeros((SUBLANES, POOL_GROUP), F32)

    def window_sum(w, cols):
        lo = w // 2
        src = lambda a, b: u_ref[a:b, cols]
        width, bufs = 1, [sa_ref, sb_ref]
        while 2 * width < w:
            dst = bufs[0]
            dst[first:stop, :] = src(first, stop) + src(first + width, stop + width)
            src = functools.partial(lambda d, a, b: d[a:b, :], dst)
            bufs.reverse()
            width *= 2
        a0 = HALO - lo
        return src(a0, a0 + tl) + src(a0 + width, a0 + width + tl)

    for gi, w in enumerate(POOL_WINDOWS):
        lo = w // 2
        hi = w - 1 - lo
        c0 = gi * POOL_GROUP
        tot = window_sum(w, slice(c0, c0 + POOL_GROUP))
        cnt = (jnp.minimum(pos + hi + 1, L) - jnp.maximum(pos - lo, 0)).astype(F32)
        dlt = tot / cnt - u_ref[HALO:HALO + tl, c0:c0 + POOL_GROUP]
        pool_ref[0, :, c0:c0 + POOL_GROUP] = (
            jnp.dot(dlt.astype(BF16), pw_ref[gi], preferred_element_type=F32)
            * ps_ref[:, c0:c0 + POOL_GROUP]).astype(pool_ref.dtype)


def _odd_mix(u, dw_w, dw_b, ln_g, ln_b, pool_w, pool_scale, tl=256, rows=64):
    bsz, L, w = u.shape
    main, left, right = _halo_specs(tl, L)
    fix2 = lambda b, i: (0, 0)
    out_spec = pl.BlockSpec((1, tl, CF_WIDTH), main)
    return pl.pallas_call(
        functools.partial(_odd_mix_body, tl=tl, L=L, rows=rows),
        grid=(bsz, L // tl),
        in_specs=[pl.BlockSpec((1, tl, w), main),
                  pl.BlockSpec((1, HALO, w), left),
                  pl.BlockSpec((1, HALO, w), right),
                  pl.BlockSpec((CF_KERNEL, CF_WIDTH), fix2),
                  pl.BlockSpec((1, CF_WIDTH), fix2), pl.BlockSpec((1, CF_WIDTH), fix2),
                  pl.BlockSpec((1, CF_WIDTH), fix2),
                  pl.BlockSpec((len(POOL_WINDOWS), POOL_GROUP, POOL_GROUP), lambda b, i: (0, 0, 0)),
                  pl.BlockSpec((1, POOL_WIDTH), fix2)],
        out_specs=[out_spec, out_spec],
        out_shape=[jax.ShapeDtypeStruct((bsz, L, CF_WIDTH), BF16)] * 2,
        scratch_shapes=[pltpu.VMEM((tl + 2 * HALO, CF_WIDTH), F32),
                        pltpu.VMEM((tl + 2 * HALO, POOL_WIDTH), F32),
                        pltpu.VMEM((tl, CF_WIDTH), F32),
                        pltpu.VMEM((SUBLANES, tl + 2 * HALO, CF_WIDTH), F32),
                        pltpu.VMEM((tl + 2 * HALO, POOL_GROUP), F32),
                        pltpu.VMEM((tl + 2 * HALO, POOL_GROUP), F32)],
        compiler_params=_params("parallel", "parallel"),
        name="odd_mix",
    )(u, u, u, dw_w, dw_b, ln_g, ln_b, pool_w, pool_scale)


def _rope_tables(L):
    inv = 1.0 / (ROPE_THETA ** (jnp.arange(0, MLA_ROPE, 2, dtype=F32) / MLA_ROPE))
    ang = jnp.arange(L, dtype=F32)[:, None] * inv[None, :]
    cos, sin = jnp.cos(ang), jnp.sin(ang)
    scale = (MLA_NOPE + MLA_ROPE) ** -0.5 * math.log2(math.e)
    ones = jnp.ones((L, MLA_NOPE), F32)
    zeros = jnp.zeros((L, MLA_NOPE), F32)
    zhalf = jnp.zeros_like(sin)
    pad = jnp.zeros((L, HEAD_PAD - MLA_NOPE - MLA_ROPE), F32)
    qc = scale * jnp.concatenate([ones, cos, cos, pad], axis=1)
    qlo = scale * jnp.concatenate([zeros, -sin, zhalf, pad], axis=1)
    qhi = scale * jnp.concatenate([zeros, zhalf, sin, pad], axis=1)
    kt = jnp.concatenate([cos, cos, sin, sin, jnp.zeros((L, LANES - 2 * MLA_ROPE), F32)], axis=1)
    return qc, qlo, qhi, kt


def _rot_half_cols(w):
    half = w.shape[-1] // 2
    return jnp.concatenate([-w[..., half:], w[..., :half]], axis=-1)


def _prep_even(ev_w_in, mla_w_uq, mla_w_ukv):
    k = ev_w_in.shape[0]
    base = 3 * HY_WIDTH + MLA_Q_RANK + MLA_KV_RANK
    kr = ev_w_in[:, base:base + MLA_ROPE]
    w_in = jnp.concatenate(
        [ev_w_in[:, :base], kr, _rot_half_cols(kr),
         jnp.zeros((k, LANES - 2 * MLA_ROPE), F32)], axis=1).astype(BF16)
    uq = mla_w_uq.reshape(MLA_Q_RANK, MLA_HEADS, MLA_NOPE + MLA_ROPE)
    zq = jnp.zeros((MLA_Q_RANK, MLA_HEADS, HEAD_PAD - MLA_NOPE - MLA_ROPE), F32)
    wq = jnp.concatenate([uq, zq], axis=2).reshape(MLA_Q_RANK, -1).astype(BF16)
    ukv = mla_w_ukv.reshape(MLA_KV_RANK, MLA_HEADS, MLA_NOPE + MLA_V)
    zk = jnp.zeros((MLA_KV_RANK, MLA_HEADS, HEAD_PAD - MLA_NOPE), F32)
    wk = jnp.concatenate([ukv[:, :, :MLA_NOPE], zk], axis=2).reshape(MLA_KV_RANK, -1).astype(BF16)
    zv = jnp.zeros((MLA_KV_RANK, MLA_HEADS, VT_ROWS - MLA_V), F32)
    wvt = jnp.concatenate([ukv[:, :, MLA_NOPE:], zv], axis=2).reshape(MLA_KV_RANK, -1).T.astype(BF16)
    return w_in, wq, wk, wvt


def _trunk(x, w, filt):
    bsz, L, d = x.shape
    t = bsz * L
    xf = x.reshape(t, d)
    qc, qlo, qhi, kt = _rope_tables(L)
    zero_b = jnp.zeros((1, d), F32)
    for layer in range(DEPTH):
        i = layer // 2
        if layer % 2 == 0:
            w_in, wq, wk, wvt = w["even"][i]
            z, x0c, p_mla = _even_in(xf.reshape(bsz, L, d), w_in, w["hy_conv_w"][i],
                                     w["hy_conv_b"][i].reshape(1, -1))
            gr, gi = filt[(L, i)]
            m1 = _hy_conv(z, x0c, gr, gi, w["hy_skip"][i], L).reshape(t, HY_WIDTH)
            q, k, vt = _mla_proj(p_mla.reshape(t, -1), L, w["mla_q_norm"][i].reshape(1, -1), wq,
                                 w["mla_kv_norm"][i].reshape(1, -1), wk, wvt, qc, qlo, qhi, kt)
            hw = MLA_HEADS * HEAD_PAD
            m2 = _attention(q.reshape(bsz, L, hw), k.reshape(bsz, L, hw), vt).reshape(t, MLA_HEADS * MLA_V)
            w_out = w["ev_w_out"][i]
            b_out = zero_b
        else:
            u = _linear(xf, w["od_w_in"][i], w["od_b_in"][i].reshape(1, -1))
            cf, pool = _odd_mix(u.reshape(bsz, L, -1), w["cf_dw_w"][i], w["cf_dw_b"][i].reshape(1, -1),
                                w["cf_ln_g"][i].reshape(1, -1), w["cf_ln_b"][i].reshape(1, -1),
                                w["pool_w"][i], w["pool_scale"][i].reshape(1, -1))
            m1 = cf.reshape(t, CF_WIDTH)
            m2 = pool.reshape(t, POOL_WIDTH)
            w_out = w["od_w_out"][i]
            b_out = w["od_b_out"][i].reshape(1, -1)
        half = w_out.shape[0] // 2
        xf = _layer_tail(m1, m2, xf, w_out[:half], w_out[half:], b_out,
                         w["ln1_g"][layer].reshape(1, -1), w["ln1_b"][layer].reshape(1, -1),
                         w["mlp_w1"][layer], w["mlp_w2"][layer],
                         w["ln2_g"][layer].reshape(1, -1), w["ln2_b"][layer].reshape(1, -1))
    return xf.reshape(bsz, L, d)


def kernel(x_prompt, x_sample, ev_w_in, hy_conv_w, hy_conv_b, hy_filt_w1, hy_filt_b1, hy_filt_w_inner, hy_filt_b_inner, hy_filt_freq, hy_filt_w_out, hy_skip, mla_q_norm, mla_w_uq, mla_kv_norm, mla_w_ukv, ev_w_out, od_w_in, od_b_in, cf_dw_w, cf_dw_b, cf_ln_g, cf_ln_b, pool_w, pool_scale, od_w_out, od_b_out, ln1_g, ln1_b, mlp_w1, mlp_w2, ln2_g, ln2_b):
    n_even = ev_w_in.shape[0]
    w = {
        "even": [_prep_even(ev_w_in[i], mla_w_uq[i], mla_w_ukv[i]) for i in range(n_even)],
        "hy_conv_w": hy_conv_w, "hy_conv_b": hy_conv_b, "hy_skip": hy_skip,
        "mla_q_norm": mla_q_norm, "mla_kv_norm": mla_kv_norm,
        "ev_w_out": ev_w_out.astype(BF16),
        "od_w_in": od_w_in.astype(BF16), "od_b_in": od_b_in,
        "cf_dw_w": cf_dw_w, "cf_dw_b": cf_dw_b, "cf_ln_g": cf_ln_g, "cf_ln_b": cf_ln_b,
        "pool_w": pool_w.astype(BF16), "pool_scale": pool_scale,
        "od_w_out": od_w_out.astype(BF16), "od_b_out": od_b_out,
        "ln1_g": ln1_g, "ln1_b": ln1_b, "ln2_g": ln2_g, "ln2_b": ln2_b,
        "mlp_w1": mlp_w1.astype(BF16), "mlp_w2": mlp_w2.astype(BF16),
    }
    filt = {}
    for L in sorted({x_prompt.shape[1], x_sample.shape[1]}):
        for i in range(n_even):
            g = _hy_filter(L, hy_filt_w1[i], hy_filt_b1[i], hy_filt_w_inner[i], hy_filt_b_inner[i],
                           hy_filt_freq[i], hy_filt_w_out[i])
            filt[(L, i)] = _hy_spectrum(g, L)
    return (_trunk(x_prompt, w, filt), _trunk(x_sample, w, filt))
```

```python
import functools
import math

import numpy as np
import jax
import jax.numpy as jnp
from jax import lax
from jax.experimental import pallas as pl
from jax.experimental.pallas import tpu as pltpu

F32 = jnp.float32
BF16 = jnp.bfloat16

D_MODEL = 1024
DEPTH = 4
HY_WIDTH = D_MODEL // 2
HY_EMB = 33
HY_BANDS = (HY_EMB - 1) // 2
HY_FILT = 64
HY_INNER = 2
HY_MAX_DECAY = math.log(1e-2) / 0.3
HY_MIN_DECAY = math.log(1e-2) / 1.5
MLA_HEADS = 8
MLA_NOPE = 64
MLA_ROPE = 32
MLA_V = 64
MLA_Q_RANK = 384
MLA_KV_RANK = 256
ROPE_THETA = 10000.0
CF_WIDTH = D_MODEL // 2
CF_KERNEL = 31
POOL_WIDTH = D_MODEL // 2
POOL_WINDOWS = (2, 4, 8, 16)
POOL_GROUP = POOL_WIDTH // len(POOL_WINDOWS)
D_FF = 4 * D_MODEL
DN_ALPHA = (2 * DEPTH) ** 0.25
LN_EPS = 1e-5
RMS_EPS = 1e-6

LANES = 128
SUBLANES = 8
VMEM_LIMIT = 52 * 1024 * 1024
HEAD_PAD = 128
VT_ROWS = 80
EVEN_COLS = 3 * HY_WIDTH + 6 * LANES
FFT_N2 = 32
FFT_PITCH = FFT_N2 + SUBLANES
FFT_MAX_UNROLL = 18
COPY_UNROLL = 8
HALO = 16


def _params(*sem):
    return pltpu.CompilerParams(dimension_semantics=sem, vmem_limit_bytes=VMEM_LIMIT)


def _layer_norm(r, g, b):
    mu = jnp.mean(r, axis=-1, keepdims=True)
    d = r - mu
    var = jnp.mean(d * d, axis=-1, keepdims=True)
    return d * lax.rsqrt(var + LN_EPS) * g + b


def _bdot(a, b):
    return jnp.dot(a.astype(BF16), b.astype(BF16), preferred_element_type=F32)


def _layer_tail_body(m1_ref, m2_ref, x_ref, wo1_ref, wo2_ref, bo_ref, g1_ref, b1_ref,
                     w1_ref, w2_ref, g2_ref, b2_ref, *rest, ff_chunk, streams):
    o_ref = rest[-2] if len(rest) == 4 else rest[0]
    rows = x_ref.shape[0] // streams
    sls = [slice(s * rows, (s + 1) * rows) for s in range(streams)]
    x1 = []
    for sl in sls:
        y = (jnp.dot(m1_ref[sl, :], wo1_ref[...], preferred_element_type=F32)
             + jnp.dot(m2_ref[sl, :], wo2_ref[...], preferred_element_type=F32) + bo_ref[...])
        x1.append(_layer_norm(DN_ALPHA * x_ref[sl, :] + y, g1_ref[...], b1_ref[...]))
    for sl, xs in zip(sls, x1):
        xb = xs.astype(BF16)
        acc = DN_ALPHA * xs
        for c in range(D_FF // ff_chunk):
            h = jnp.dot(xb, w1_ref[:, c * ff_chunk:(c + 1) * ff_chunk], preferred_element_type=F32)
            h = jnp.maximum(h, 0.0)
            acc = acc + jnp.dot((h * h).astype(BF16), w2_ref[c * ff_chunk:(c + 1) * ff_chunk, :],
                                preferred_element_type=F32)
        out = _layer_norm(acc, g2_ref[...], b2_ref[...])
        o_ref[sl, :] = out
        if len(rest) == 4:
            wn_ref, bn_ref, _, u_ref = rest
            u_ref[sl, :] = _bdot(out, wn_ref[...]) + bn_ref[...]


def _layer_tail(m1, m2, x, wo1, wo2, bo, g1, b1, w1, w2, g2, b2, w_next=None, b_next=None,
                tm=512, ff_chunk=1024, streams=2):
    t, d = x.shape
    k = m1.shape[1]
    row = lambda i: (i, 0)
    fix = lambda i: (0, 0)
    resident = lambda shape: pl.BlockSpec(shape, fix, pipeline_mode=pl.Buffered(1))
    vec = pl.BlockSpec((1, d), fix)
    in_specs = [pl.BlockSpec((tm, k), row), pl.BlockSpec((tm, k), row), pl.BlockSpec((tm, d), row),
                resident((k, d)), resident((k, d)), vec, vec, vec,
                resident((d, D_FF)), resident((D_FF, d)), vec, vec]
    out_specs = [pl.BlockSpec((tm, d), row)]
    out_shape = [jax.ShapeDtypeStruct((t, d), F32)]
    args = [m1, m2, x, wo1, wo2, bo, g1, b1, w1, w2, g2, b2]
    if w_next is not None:
        n = w_next.shape[1]
        in_specs += [resident((d, n)), pl.BlockSpec((1, n), fix)]
        out_specs.append(pl.BlockSpec((tm, n), row))
        out_shape.append(jax.ShapeDtypeStruct((t, n), F32))
        args += [w_next, b_next]
    outs = pl.pallas_call(
        functools.partial(_layer_tail_body, ff_chunk=ff_chunk, streams=streams),
        grid=(t // tm,),
        in_specs=in_specs,
        out_specs=out_specs,
        out_shape=out_shape,
        compiler_params=_params("parallel"),
        name="layer_tail",
    )(*args)
    return outs if w_next is not None else outs[0]


def _halo_specs(tl, L):
    per = tl // HALO
    nblk = L // HALO

    def main(b, i):
        return (b, i, 0)

    def left(b, i):
        return (b, jnp.maximum(i * per - 1, 0), 0)

    def right(b, i):
        return (b, jnp.minimum((i + 1) * per, nblk - 1), 0)

    return main, left, right


def _even_in_body(xm_ref, xl_ref, xr_ref, w_ref, cw_ref, cb_ref, z_ref, x0_ref, pm_ref, ext_ref, *, tl):
    i = pl.program_id(1)
    last = pl.num_programs(1) - 1
    hw = 3 * HY_WIDTH
    xm = xm_ref[0].astype(BF16)
    xl = xl_ref[0].astype(BF16)
    xr = xr_ref[0].astype(BF16)

    def project(part):
        cols = slice(part * HY_WIDTH, (part + 1) * HY_WIDTH)
        ext_ref[0:HALO, cols] = jnp.where(i == 0, 0.0, _bdot(xl, w_ref[:, cols]))
        ext_ref[HALO:HALO + tl, cols] = _bdot(xm, w_ref[:, cols])
        ext_ref[HALO + tl:, cols] = jnp.where(i == last, 0.0, _bdot(xr, w_ref[:, cols]))

    def conv(part):
        cols = slice(part * HY_WIDTH, (part + 1) * HY_WIDTH)
        return (cw_ref[0:1, cols] * ext_ref[HALO - 1:HALO - 1 + tl, cols]
                + cw_ref[1:2, cols] * ext_ref[HALO:HALO + tl, cols]
                + cw_ref[2:3, cols] * ext_ref[HALO + 1:HALO + 1 + tl, cols] + cb_ref[:, cols])

    project(0)
    project(1)
    x0_ref[0] = conv(0)
    project(2)
    x1c = conv(1)
    pm_ref[0] = _bdot(xm, w_ref[:, hw:])
    z_ref[0] = x1c * conv(2)


def _even_in(x, w_in, conv_w, conv_b, tl=512):
    bsz, L, d = x.shape
    hw = 3 * HY_WIDTH
    rest = EVEN_COLS - hw
    main, left, right = _halo_specs(tl, L)
    fix = lambda b, i: (0, 0)
    hy_spec = pl.BlockSpec((1, tl, HY_WIDTH), main)
    return pl.pallas_call(
        functools.partial(_even_in_body, tl=tl),
        grid=(bsz, L // tl),
        in_specs=[pl.BlockSpec((1, tl, d), main), pl.BlockSpec((1, HALO, d), left),
                  pl.BlockSpec((1, HALO, d), right),
                  pl.BlockSpec((d, EVEN_COLS), fix), pl.BlockSpec((3, hw), fix), pl.BlockSpec((1, hw), fix)],
        out_specs=[hy_spec, hy_spec, pl.BlockSpec((1, tl, rest), main)],
        out_shape=[jax.ShapeDtypeStruct((bsz, L, HY_WIDTH), F32)] * 2
        + [jax.ShapeDtypeStruct((bsz, L, rest), F32)],
        scratch_shapes=[pltpu.VMEM((tl + 2 * HALO, hw), F32)],
        compiler_params=_params("parallel", "parallel"),
        name="even_in",
    )(x, x, x, w_in, conv_w, conv_b)


def _hy_filter_body(wt_ref, wc_ref, ws_ref, b1_ref, wi_ref, bi_ref, fr_ref, wo_ref, g_ref, *, L, tr):
    hi = lax.Precision.HIGHEST
    dot = functools.partial(jnp.dot, precision=hi, preferred_element_type=F32)
    scale_t = 1.0 / (L - 1)
    m_row = pl.program_id(0) * tr + lax.broadcasted_iota(jnp.int32, (1, tr), 1)
    j_row = jnp.where(m_row < L, m_row, 2 * L - m_row).astype(F32)
    band_step = (HY_BANDS - 1 - 1e-4) / (HY_BANDS - 1)
    bands = 1e-4 + band_step * lax.broadcasted_iota(jnp.int32, (HY_BANDS, 1), 0).astype(F32)
    arg = bands * (j_row * (2.0 * math.pi / L))
    freq = fr_ref[...]
    pre = (wt_ref[...] * (j_row * scale_t) + dot(wc_ref[...], jnp.cos(arg))
           - dot(ws_ref[...], jnp.sin(arg)) + b1_ref[...])
    h = jnp.sin(freq * pre)
    for l in range(HY_INNER):
        h = jnp.sin(freq * (dot(wi_ref[l], h) + bi_ref[l]))
    k = dot(h.T, wo_ref[...])
    m = pl.program_id(0) * tr + lax.broadcasted_iota(jnp.int32, (tr, 1), 0)
    t = jnp.where(m < L, m, 2 * L - m).astype(F32) * scale_t
    ch = lax.broadcasted_iota(jnp.int32, (1, HY_WIDTH), 1).astype(F32)
    deltas = jnp.abs(HY_MIN_DECAY + ch * ((HY_MAX_DECAY - HY_MIN_DECAY) / (HY_WIDTH - 1)))
    g_ref[...] = jnp.where(m == L, 0.0, k * jnp.exp(-t * deltas))


def _hy_filter(L, w1, b1, w_inner, b_inner, freq, w_out, tr=512):
    n = 2 * L
    full = lambda *shape: pl.BlockSpec(shape, lambda i: (0,) * len(shape))
    col = lambda a: a.reshape(-1, 1)
    w1t = w1.T
    return pl.pallas_call(
        functools.partial(_hy_filter_body, L=L, tr=tr),
        grid=(n // tr,),
        in_specs=[full(HY_FILT, 1), full(HY_FILT, HY_BANDS), full(HY_FILT, HY_BANDS), full(HY_FILT, 1),
                  full(HY_INNER, HY_FILT, HY_FILT), full(HY_INNER, HY_FILT, 1), full(HY_FILT, 1),
                  pl.BlockSpec((HY_FILT, HY_WIDTH), lambda i: (0, i // (L // tr)))],
        out_specs=pl.BlockSpec((tr, HY_WIDTH), lambda i: (i, 0)),
        out_shape=jax.ShapeDtypeStruct((n, HY_WIDTH), F32),
        compiler_params=_params("parallel"),
        name="hy_filter",
    )(w1t[:, 0:1], w1t[:, 1:1 + HY_BANDS], w1t[:, 1 + HY_BANDS:], col(b1),
      jnp.swapaxes(w_inner, 1, 2), b_inner.reshape(HY_INNER, HY_FILT, 1), col(freq), w_out)


def _fft_dims(L):
    n = 2 * L
    n1 = n // FFT_N2
    return n, n1, n1 // 2 + SUBLANES, n1 // 2


@functools.lru_cache(maxsize=None)
def _fft_tables(L):
    n, n1, kh, h1 = _fft_dims(L)
    n2 = FFT_N2
    k1 = np.arange(kh, dtype=np.float64)[:, None]
    t1 = np.arange(n1, dtype=np.float64)[None, :]
    th = 2.0 * np.pi * k1 * t1 / n1
    fy = np.concatenate([np.cos(th), -np.sin(th)], axis=0)
    k2 = np.arange(n2, dtype=np.float64)[None, :, None]
    t2 = np.arange(n2, dtype=np.float64)[None, None, :]
    ph = 2.0 * np.pi * (k2 * t2 / n2 + k1[:, :, None] * t2 / n)
    mr, mi = np.cos(ph), -np.sin(ph)
    mx = np.concatenate([np.concatenate([mr, -mi], axis=2),
                         np.concatenate([mi, mr], axis=2)], axis=1)
    mxi = np.transpose(mx, (0, 2, 1))
    wk = np.zeros((kh,), np.float64)
    wk[:n1 // 2 + 1] = 2.0
    wk[0] = 1.0
    wk[n1 // 2] = 1.0
    thi = th[:, :h1].T
    fyi = np.concatenate([np.cos(thi) * wk[None, :], -np.sin(thi) * wk[None, :]], axis=1)
    return fy, mx, mxi, fyi


def _fft_unroll(kh):
    return max(u for u in range(1, FFT_MAX_UNROLL + 1) if kh % u == 0)


def _lane_cat(parts):
    return parts[0] if len(parts) == 1 else jnp.concatenate(parts, axis=1)


def _pitch_rows(load, dst_ref, nblk, nb):
    def body(t1, carry):
        s = pl.multiple_of(t1 * FFT_N2, SUBLANES)
        d = pl.multiple_of(t1 * FFT_PITCH, SUBLANES)
        for e in range(nb):
            dst_ref[e, pl.ds(d, FFT_N2), :] = load(e, s)
        return carry
    lax.fori_loop(0, nblk, body, 0, unroll=COPY_UNROLL)


def _fft_stage_y(src_ref, fy_ref, ar_ref, ai_ref, nblk, kh, nb):
    fy = fy_ref[...]
    for t2 in range(FFT_N2):
        x = _lane_cat([src_ref[e, pl.ds(t2, nblk, stride=FFT_PITCH), :] for e in range(nb)]).astype(BF16)
        a = jnp.dot(fy, x, preferred_element_type=F32)
        for e in range(nb):
            ar_ref[e, pl.ds(t2, kh, stride=FFT_PITCH), :] = a[:kh, e * LANES:(e + 1) * LANES]
            ai_ref[e, pl.ds(t2, kh, stride=FFT_PITCH), :] = a[kh:, e * LANES:(e + 1) * LANES]


def _hy_spectrum_body(g_ref, fy_ref, mx_ref, gr_ref, gi_ref, gs_ref, ar_ref, ai_ref, *, L):
    n, n1, kh, _ = _fft_dims(L)
    _pitch_rows(lambda e, s: g_ref[pl.ds(s, FFT_N2), :], gs_ref, n1, 1)
    _fft_stage_y(gs_ref, fy_ref, ar_ref, ai_ref, n1, kh, 1)
    unroll = _fft_unroll(kh)

    def body(it, carry):
        k1s = [it * unroll + u for u in range(unroll)]
        a = []
        for k1 in k1s:
            src = pl.multiple_of(k1 * FFT_PITCH, SUBLANES)
            a.append(jnp.concatenate([ar_ref[0, pl.ds(src, FFT_N2), :], ai_ref[0, pl.ds(src, FFT_N2), :]],
                                     axis=0).astype(BF16))
        z = [jnp.dot(mx_ref[k1], x, preferred_element_type=F32) * (1.0 / n) for k1, x in zip(k1s, a)]
        for k1, zz in zip(k1s, z):
            dst = pl.multiple_of(k1 * FFT_N2, SUBLANES)
            gr_ref[pl.ds(dst, FFT_N2), :] = zz[:FFT_N2]
            gi_ref[pl.ds(dst, FFT_N2), :] = zz[FFT_N2:]
        return carry
    lax.fori_loop(0, kh // unroll, body, 0)


def _hy_spectrum(g, L):
    n, n1, kh, _ = _fft_dims(L)
    fy, mx, _, _ = _fft_tables(L)
    out_spec = pl.BlockSpec((kh * FFT_N2, LANES), lambda c: (0, c))
    return pl.pallas_call(
        functools.partial(_hy_spectrum_body, L=L),
        grid=(HY_WIDTH // LANES,),
        in_specs=[pl.BlockSpec((n, LANES), lambda c: (0, c)),
                  pl.BlockSpec((2 * kh, n1), lambda c: (0, 0)),
                  pl.BlockSpec((kh, 2 * FFT_N2, 2 * FFT_N2), lambda c: (0, 0, 0))],
        out_specs=[out_spec, out_spec],
        out_shape=[jax.ShapeDtypeStruct((kh * FFT_N2, HY_WIDTH), F32)] * 2,
        scratch_shapes=[pltpu.VMEM((1, n1 * FFT_PITCH, LANES), F32),
                        pltpu.VMEM((1, kh * FFT_PITCH, LANES), F32),
                        pltpu.VMEM((1, kh * FFT_PITCH, LANES), F32)],
        compiler_params=_params("parallel"),
        name="hy_spectrum",
    )(g, jnp.asarray(fy, BF16), jnp.asarray(mx, BF16))


def _hy_conv_body(z_ref, x0_ref, gr_ref, gi_ref, fy_ref, mx_ref, mxi_ref, fyi_ref, skip_ref, o_ref,
                  zs_ref, ar_ref, ai_ref, *, L, nb):
    n, n1, kh, h1 = _fft_dims(L)
    seqs = range(nb)
    _pitch_rows(lambda e, s: z_ref[e, pl.ds(s, FFT_N2), :], zs_ref, h1, nb)
    _fft_stage_y(zs_ref, fy_ref, ar_ref, ai_ref, h1, kh, nb)

    unroll = _fft_unroll(kh)

    def spectral(it, carry):
        k1s = [it * unroll + u for u in range(unroll)]
        rows = [pl.multiple_of(k1 * FFT_PITCH, SUBLANES) for k1 in k1s]
        grows = [pl.multiple_of(k1 * FFT_N2, SUBLANES) for k1 in k1s]
        a = [_lane_cat([jnp.concatenate([ar_ref[e, pl.ds(r, FFT_N2), :], ai_ref[e, pl.ds(r, FFT_N2), :]], axis=0)
                        for e in seqs]).astype(BF16) for r in rows]
        z = [jnp.dot(mx_ref[k1], x, preferred_element_type=F32) for k1, x in zip(k1s, a)]
        y = []
        for zz, g in zip(z, grows):
            zr, zi = zz[:FFT_N2], zz[FFT_N2:]
            gr = _lane_cat([gr_ref[pl.ds(g, FFT_N2), :]] * nb)
            gi = _lane_cat([gi_ref[pl.ds(g, FFT_N2), :]] * nb)
            y.append(jnp.concatenate([zr * gr - zi * gi, zr * gi + zi * gr], axis=0).astype(BF16))
        b = [jnp.dot(mxi_ref[k1], x, preferred_element_type=F32) for k1, x in zip(k1s, y)]
        for r, bb in zip(rows, b):
            for e in seqs:
                ar_ref[e, pl.ds(r, FFT_N2), :] = bb[:FFT_N2, e * LANES:(e + 1) * LANES]
                ai_ref[e, pl.ds(r, FFT_N2), :] = bb[FFT_N2:, e * LANES:(e + 1) * LANES]
        return carry
    lax.fori_loop(0, kh // unroll, spectral, 0)

    fyi = fyi_ref[...]
    for t2 in range(FFT_N2):
        b = _lane_cat([jnp.concatenate([ar_ref[e, pl.ds(t2, kh, stride=FFT_PITCH), :],
                                        ai_ref[e, pl.ds(t2, kh, stride=FFT_PITCH), :]], axis=0) for e in seqs])
        y = jnp.dot(fyi, b.astype(BF16), preferred_element_type=F32)
        for e in seqs:
            zs_ref[e, pl.ds(t2, h1, stride=FFT_PITCH), :] = y[:, e * LANES:(e + 1) * LANES]

    skip = skip_ref[...]

    def finish(t1, carry):
        s = pl.multiple_of(t1 * FFT_N2, SUBLANES)
        d = pl.multiple_of(t1 * FFT_PITCH, SUBLANES)
        for e in seqs:
            y = zs_ref[e, pl.ds(d, FFT_N2), :] + z_ref[e, pl.ds(s, FFT_N2), :] * skip
            o_ref[e, pl.ds(s, FFT_N2), :] = (x0_ref[e, pl.ds(s, FFT_N2), :] * y).astype(o_ref.dtype)
        return carry
    lax.fori_loop(0, h1, finish, 0, unroll=COPY_UNROLL)


def _hy_conv(z, x0c, gr, gi, skip, L, nb=2):
    bsz = z.shape[0]
    n, n1, kh, h1 = _fft_dims(L)
    fy, mx, mxi, fyi = _fft_tables(L)
    seq = pl.BlockSpec((nb, L, LANES), lambda c, b: (b, 0, c))
    fixed = lambda shape, imap: pl.BlockSpec(shape, imap, pipeline_mode=pl.Buffered(1))
    spec = fixed((kh * FFT_N2, LANES), lambda c, b: (0, c))
    mat = fixed((kh, 2 * FFT_N2, 2 * FFT_N2), lambda c, b: (0, 0, 0))
    return pl.pallas_call(
        functools.partial(_hy_conv_body, L=L, nb=nb),
        grid=(HY_WIDTH // LANES, bsz // nb),
        in_specs=[seq, seq, spec, spec,
                  pl.BlockSpec((2 * kh, h1), lambda c, b: (0, 0)),
                  mat, mat,
                  pl.BlockSpec((h1, 2 * kh), lambda c, b: (0, 0)),
                  pl.BlockSpec((1, LANES), lambda c, b: (0, c))],
        out_specs=seq,
        out_shape=jax.ShapeDtypeStruct((bsz, L, HY_WIDTH), BF16),
        scratch_shapes=[pltpu.VMEM((nb, h1 * FFT_PITCH, LANES), F32),
                        pltpu.VMEM((nb, kh * FFT_PITCH, LANES), F32),
                        pltpu.VMEM((nb, kh * FFT_PITCH, LANES), F32)],
        compiler_params=_params("parallel", "parallel"),
        name="hy_conv",
    )(z, x0c, gr, gi, jnp.asarray(fy[:, :h1], BF16), jnp.asarray(mx, BF16), jnp.asarray(mxi, BF16),
      jnp.asarray(fyi, BF16), skip.reshape(1, -1))


def _rms(x, g):
    return x * lax.rsqrt(jnp.mean(x * x, axis=-1, keepdims=True) + RMS_EPS) * g


def _mla_proj_body(p_ref, qn_ref, wq_ref, kvn_ref, wk_ref, wvt_ref,
                   qc_ref, qlo_ref, qhi_ref, kt_ref, q_ref, k_ref, vt_ref):
    p = p_ref[...]
    cq = _rms(p[:, :MLA_Q_RANK], qn_ref[...]).astype(BF16)
    ckv = _rms(p[:, MLA_Q_RANK:MLA_Q_RANK + MLA_KV_RANK], kvn_ref[...]).astype(BF16)
    q = jnp.dot(cq, wq_ref[...], preferred_element_type=F32)
    k = jnp.dot(ckv, wk_ref[...], preferred_element_type=F32)
    kr = p[:, MLA_Q_RANK + MLA_KV_RANK:] * kt_ref[...]
    kr = kr + pltpu.roll(kr, LANES - MLA_ROPE, axis=1)
    lane = lax.broadcasted_iota(jnp.int32, (1, HEAD_PAD), 1)
    in_rope = (lane >= MLA_NOPE) & (lane < MLA_NOPE + MLA_ROPE)
    kr = jnp.where(in_rope, pltpu.roll(kr, MLA_NOPE, axis=1), 0.0)
    qc = qc_ref[...]
    qlo = qlo_ref[...]
    qhi = qhi_ref[...]
    half = MLA_ROPE // 2
    for h in range(MLA_HEADS):
        sl = slice(h * HEAD_PAD, (h + 1) * HEAD_PAD)
        x = q[:, sl]
        q_ref[:, sl] = (x * qc + pltpu.roll(x, HEAD_PAD - half, axis=1) * qlo
                        + pltpu.roll(x, half, axis=1) * qhi).astype(BF16)
        k_ref[:, sl] = (k[:, sl] + kr).astype(BF16)
    vt = lax.dot_general(wvt_ref[...], ckv, (((1,), (1,)), ((), ())), preferred_element_type=F32)
    row = lax.broadcasted_iota(jnp.int32, (MLA_HEADS * VT_ROWS, 1), 0) % VT_ROWS
    vt_ref[0] = (vt + jnp.where(row == MLA_V, 1.0, 0.0)).astype(BF16)


def _mla_proj(p, L, q_norm, wq, kv_norm, wk, wvt, qc, qlo, qhi, kt, tm=512):
    t = p.shape[0]
    hw = MLA_HEADS * HEAD_PAD
    nper = L // tm
    fix = lambda i: (0, 0)
    pos = lambda i: (i % nper, 0)
    row = lambda i: (i, 0)
    wide = 6 * LANES
    return pl.pallas_call(
        _mla_proj_body,
        grid=(t // tm,),
        in_specs=[pl.BlockSpec((tm, wide), row),
                  pl.BlockSpec((1, MLA_Q_RANK), fix),
                  pl.BlockSpec((MLA_Q_RANK, hw), fix),
                  pl.BlockSpec((1, MLA_KV_RANK), fix),
                  pl.BlockSpec((MLA_KV_RANK, hw), fix),
                  pl.BlockSpec((MLA_HEADS * VT_ROWS, MLA_KV_RANK), fix),
                  pl.BlockSpec((tm, HEAD_PAD), pos), pl.BlockSpec((tm, HEAD_PAD), pos),
                  pl.BlockSpec((tm, HEAD_PAD), pos), pl.BlockSpec((tm, LANES), pos)],
        out_specs=[pl.BlockSpec((tm, hw), row), pl.BlockSpec((tm, hw), row),
                   pl.BlockSpec((1, MLA_HEADS * VT_ROWS, tm), lambda i: (i // nper, 0, i % nper))],
        out_shape=[jax.ShapeDtypeStruct((t, hw), BF16), jax.ShapeDtypeStruct((t, hw), BF16),
                   jax.ShapeDtypeStruct((t // L, MLA_HEADS * VT_ROWS, L), BF16)],
        compiler_params=_params("parallel"),
        name="mla_proj",
    )(p, q_norm, wq, kv_norm, wk, wvt, qc, qlo, qhi, kt)


def _attn_body(q_ref, k_ref, vt_ref, o_ref, *, kc):
    L = k_ref.shape[1]
    nchunk = L // kc
    heads = range(2)
    qs = [q_ref[0, :, h * HEAD_PAD:(h + 1) * HEAD_PAD] for h in heads]

    def scores(h, c):
        return lax.dot_general(k_ref[0, c * kc:(c + 1) * kc, h * HEAD_PAD:(h + 1) * HEAD_PAD], qs[h],
                               (((1,), (1,)), ((), ())), preferred_element_type=F32)

    m = [None, None]
    acc = [None, None]
    s_next = [scores(h, 0) for h in heads]
    for c in range(nchunk):
        s_cur = s_next
        if c + 1 < nchunk:
            s_next = [scores(h, c + 1) for h in heads]
        for h in heads:
            s = s_cur[h]
            cmax = jnp.max(s, axis=0, keepdims=True)
            m_new = cmax if c == 0 else jnp.maximum(m[h], cmax)
            e = jnp.exp2(s - m_new).astype(BF16)
            pv = jnp.dot(vt_ref[0, h * VT_ROWS:(h + 1) * VT_ROWS, c * kc:(c + 1) * kc], e,
                         preferred_element_type=F32)
            acc[h] = pv if c == 0 else acc[h] * jnp.exp2(m[h] - m_new) + pv
            m[h] = m_new
    out_t = jnp.concatenate([acc[h][:MLA_V] / acc[h][MLA_V:MLA_V + 1] for h in heads], axis=0)
    o_ref[0] = out_t.T.astype(o_ref.dtype)


def _attention(q, k, vt, tq=1024, kc=256):
    bsz, L, _ = q.shape
    return pl.pallas_call(
        functools.partial(_attn_body, kc=kc),
        grid=(bsz, MLA_HEADS // 2, L // tq),
        in_specs=[pl.BlockSpec((1, tq, 2 * HEAD_PAD), lambda b, g, i: (b, i, g)),
                  pl.BlockSpec((1, L, 2 * HEAD_PAD), lambda b, g, i: (b, 0, g)),
                  pl.BlockSpec((1, 2 * VT_ROWS, L), lambda b, g, i: (b, g, 0))],
        out_specs=pl.BlockSpec((1, tq, 2 * MLA_V), lambda b, g, i: (b, i, g)),
        out_shape=jax.ShapeDtypeStruct((bsz, L, MLA_HEADS * MLA_V), BF16),
        compiler_params=_params("parallel", "parallel", "parallel"),
        name="attention",
    )(q, k, vt)


def _odd_mix_body(um_ref, ul_ref, ur_ref, dw_ref, db_ref, lg_ref, lb_ref, pw_ref, ps_ref,
                  cf_ref, pool_ref, h_ref, u_ref, conv_ref, hp_ref, sa_ref, sb_ref, *, tl, L, rows):
    i = pl.program_id(1)
    last = pl.num_programs(1) - 1

    def glu(u):
        return u[:, :CF_WIDTH] * jax.nn.sigmoid(u[:, CF_WIDTH:2 * CF_WIDTH])

    ul = ul_ref[0]
    um = um_ref[0]
    ur = ur_ref[0]
    h_ref[0:HALO, :] = jnp.where(i == 0, 0.0, glu(ul))
    h_ref[HALO:HALO + tl, :] = glu(um)
    h_ref[HALO + tl:, :] = jnp.where(i == last, 0.0, glu(ur))
    u_ref[0:HALO, :] = jnp.where(i == 0, 0.0, ul[:, 2 * CF_WIDTH:])
    u_ref[HALO:HALO + tl, :] = um[:, 2 * CF_WIDTH:]
    u_ref[HALO + tl:, :] = jnp.where(i == last, 0.0, ur[:, 2 * CF_WIDTH:])

    span = tl + 2 * HALO - SUBLANES
    for j in range(SUBLANES):
        hp_ref[j, 0:span, :] = h_ref[j:j + span, :]

    off = HALO - CF_KERNEL // 2
    for r in range(0, tl, rows):
        for c in range(0, CF_WIDTH, LANES):
            acc = jnp.zeros((rows, LANES), F32) + db_ref[:, c:c + LANES]
            for k in range(CF_KERNEL):
                j = (off + k) % SUBLANES
                base = r + off + k - j
                acc = acc + dw_ref[k:k + 1, c:c + LANES] * hp_ref[j, base:base + rows, c:c + LANES]
            conv_ref[r:r + rows, c:c + LANES] = acc
    hn = _layer_norm(conv_ref[...], lg_ref[...], lb_ref[...])
    cf_ref[0] = (hn * jax.nn.sigmoid(hn)).astype(cf_ref.dtype)

    pos = i * tl + lax.broadcasted_iota(jnp.int32, (tl, 1), 0)
    ext = tl + 2 * HALO
    first, stop = SUBLANES, ext - SUBLANES
    for buf in (sa_ref, sb_ref):
        buf[stop:ext, :] = jnp.zeros((SUBLANES, POOL_GROUP), F32)

    def window_sum(w, cols):
        lo = w // 2
        src = lambda a, b: u_ref[a:b, cols]
        width, bufs = 1, [sa_ref, sb_ref]
        while 2 * width < w:
            dst = bufs[0]
            dst[first:stop, :] = src(first, stop) + src(first + width, stop + width)
            src = functools.partial(lambda d, a, b: d[a:b, :], dst)
            bufs.reverse()
            width *= 2
        a0 = HALO - lo
        return src(a0, a0 + tl) + src(a0 + width, a0 + width + tl)

    for gi, w in enumerate(POOL_WINDOWS):
        lo = w // 2
        hi = w - 1 - lo
        c0 = gi * POOL_GROUP
        tot = window_sum(w, slice(c0, c0 + POOL_GROUP))
        cnt = (jnp.minimum(pos + hi + 1, L) - jnp.maximum(pos - lo, 0)).astype(F32)
        dlt = tot / cnt - u_ref[HALO:HALO + tl, c0:c0 + POOL_GROUP]
        pool_ref[0, :, c0:c0 + POOL_GROUP] = (
            jnp.dot(dlt.astype(BF16), pw_ref[gi], preferred_element_type=F32)
            * ps_ref[:, c0:c0 + POOL_GROUP]).astype(pool_ref.dtype)


def _odd_mix(u, dw_w, dw_b, ln_g, ln_b, pool_w, pool_scale, tl=512, rows=128):
    bsz, L, w = u.shape
    main, left, right = _halo_specs(tl, L)
    fix2 = lambda b, i: (0, 0)
    out_spec = pl.BlockSpec((1, tl, CF_WIDTH), main)
    return pl.pallas_call(
        functools.partial(_odd_mix_body, tl=tl, L=L, rows=rows),
        grid=(bsz, L // tl),
        in_specs=[pl.BlockSpec((1, tl, w), main),
                  pl.BlockSpec((1, HALO, w), left),
                  pl.BlockSpec((1, HALO, w), right),
                  pl.BlockSpec((CF_KERNEL, CF_WIDTH), fix2),
                  pl.BlockSpec((1, CF_WIDTH), fix2), pl.BlockSpec((1, CF_WIDTH), fix2),
                  pl.BlockSpec((1, CF_WIDTH), fix2),
                  pl.BlockSpec((len(POOL_WINDOWS), POOL_GROUP, POOL_GROUP), lambda b, i: (0, 0, 0)),
                  pl.BlockSpec((1, POOL_WIDTH), fix2)],
        out_specs=[out_spec, out_spec],
        out_shape=[jax.ShapeDtypeStruct((bsz, L, CF_WIDTH), BF16)] * 2,
        scratch_shapes=[pltpu.VMEM((tl + 2 * HALO, CF_WIDTH), F32),
                        pltpu.VMEM((tl + 2 * HALO, POOL_WIDTH), F32),
                        pltpu.VMEM((tl, CF_WIDTH), F32),
                        pltpu.VMEM((SUBLANES, tl + 2 * HALO, CF_WIDTH), F32),
                        pltpu.VMEM((tl + 2 * HALO, POOL_GROUP), F32),
                        pltpu.VMEM((tl + 2 * HALO, POOL_GROUP), F32)],
        compiler_params=_params("parallel", "parallel"),
        name="odd_mix",
    )(u, u, u, dw_w, dw_b, ln_g, ln_b, pool_w, pool_scale)


def _rope_tables(L):
    inv = 1.0 / (ROPE_THETA ** (jnp.arange(0, MLA_ROPE, 2, dtype=F32) / MLA_ROPE))
    ang = jnp.arange(L, dtype=F32)[:, None] * inv[None, :]
    cos, sin = jnp.cos(ang), jnp.sin(ang)
    scale = (MLA_NOPE + MLA_ROPE) ** -0.5 * math.log2(math.e)
    ones = jnp.ones((L, MLA_NOPE), F32)
    zeros = jnp.zeros((L, MLA_NOPE), F32)
    zhalf = jnp.zeros_like(sin)
    pad = jnp.zeros((L, HEAD_PAD - MLA_NOPE - MLA_ROPE), F32)
    qc = scale * jnp.concatenate([ones, cos, cos, pad], axis=1)
    qlo = scale * jnp.concatenate([zeros, -sin, zhalf, pad], axis=1)
    qhi = scale * jnp.concatenate([zeros, zhalf, sin, pad], axis=1)
    kt = jnp.concatenate([cos, cos, sin, sin, jnp.zeros((L, LANES - 2 * MLA_ROPE), F32)], axis=1)
    return qc, qlo, qhi, kt


def _rot_half_cols(w):
    half = w.shape[-1] // 2
    return jnp.concatenate([-w[..., half:], w[..., :half]], axis=-1)


def _prep_even(ev_w_in, mla_w_uq, mla_w_ukv):
    k = ev_w_in.shape[0]
    base = 3 * HY_WIDTH + MLA_Q_RANK + MLA_KV_RANK
    kr = ev_w_in[:, base:base + MLA_ROPE]
    w_in = jnp.concatenate(
        [ev_w_in[:, :base], kr, _rot_half_cols(kr),
         jnp.zeros((k, LANES - 2 * MLA_ROPE), F32)], axis=1).astype(BF16)
    uq = mla_w_uq.reshape(MLA_Q_RANK, MLA_HEADS, MLA_NOPE + MLA_ROPE)
    zq = jnp.zeros((MLA_Q_RANK, MLA_HEADS, HEAD_PAD - MLA_NOPE - MLA_ROPE), F32)
    wq = jnp.concatenate([uq, zq], axis=2).reshape(MLA_Q_RANK, -1).astype(BF16)
    ukv = mla_w_ukv.reshape(MLA_KV_RANK, MLA_HEADS, MLA_NOPE + MLA_V)
    zk = jnp.zeros((MLA_KV_RANK, MLA_HEADS, HEAD_PAD - MLA_NOPE), F32)
    wk = jnp.concatenate([ukv[:, :, :MLA_NOPE], zk], axis=2).reshape(MLA_KV_RANK, -1).astype(BF16)
    zv = jnp.zeros((MLA_KV_RANK, MLA_HEADS, VT_ROWS - MLA_V), F32)
    wvt = jnp.concatenate([ukv[:, :, MLA_NOPE:], zv], axis=2).reshape(MLA_KV_RANK, -1).T.astype(BF16)
    return w_in, wq, wk, wvt


def _trunk(x, w, filt):
    bsz, L, d = x.shape
    t = bsz * L
    xf = x.reshape(t, d)
    qc, qlo, qhi, kt = _rope_tables(L)
    zero_b = jnp.zeros((1, d), F32)
    for layer in range(DEPTH):
        i = layer // 2
        if layer % 2 == 0:
            w_in, wq, wk, wvt = w["even"][i]
            z, x0c, p_mla = _even_in(xf.reshape(bsz, L, d), w_in, w["hy_conv_w"][i],
                                     w["hy_conv_b"][i].reshape(1, -1))
            gr, gi = filt[(L, i)]
            m1 = _hy_conv(z, x0c, gr, gi, w["hy_skip"][i], L).reshape(t, HY_WIDTH)
            q, k, vt = _mla_proj(p_mla.reshape(t, -1), L, w["mla_q_norm"][i].reshape(1, -1), wq,
                                 w["mla_kv_norm"][i].reshape(1, -1), wk, wvt, qc, qlo, qhi, kt)
            hw = MLA_HEADS * HEAD_PAD
            m2 = _attention(q.reshape(bsz, L, hw), k.reshape(bsz, L, hw), vt).reshape(t, MLA_HEADS * MLA_V)
            w_out = w["ev_w_out"][i]
            b_out = zero_b
        else:
            cf, pool = _odd_mix(u.reshape(bsz, L, -1), w["cf_dw_w"][i], w["cf_dw_b"][i].reshape(1, -1),
                                w["cf_ln_g"][i].reshape(1, -1), w["cf_ln_b"][i].reshape(1, -1),
                                w["pool_w"][i], w["pool_scale"][i].reshape(1, -1))
            m1 = cf.reshape(t, CF_WIDTH)
            m2 = pool.reshape(t, POOL_WIDTH)
            w_out = w["od_w_out"][i]
            b_out = w["od_b_out"][i].reshape(1, -1)
        half = w_out.shape[0] // 2
        tail_args = (m1, m2, xf, w_out[:half], w_out[half:], b_out,
                     w["ln1_g"][layer].reshape(1, -1), w["ln1_b"][layer].reshape(1, -1),
                     w["mlp_w1"][layer], w["mlp_w2"][layer],
                     w["ln2_g"][layer].reshape(1, -1), w["ln2_b"][layer].reshape(1, -1))
        if layer % 2 == 0 and layer + 1 < DEPTH:
            xf, u = _layer_tail(*tail_args, w["od_w_in"][i], w["od_b_in"][i].reshape(1, -1))
        else:
            xf = _layer_tail(*tail_args)
    return xf.reshape(bsz, L, d)


def kernel(x_prompt, x_sample, ev_w_in, hy_conv_w, hy_conv_b, hy_filt_w1, hy_filt_b1, hy_filt_w_inner, hy_filt_b_inner, hy_filt_freq, hy_filt_w_out, hy_skip, mla_q_norm, mla_w_uq, mla_kv_norm, mla_w_ukv, ev_w_out, od_w_in, od_b_in, cf_dw_w, cf_dw_b, cf_ln_g, cf_ln_b, pool_w, pool_scale, od_w_out, od_b_out, ln1_g, ln1_b, mlp_w1, mlp_w2, ln2_g, ln2_b):
    n_even = ev_w_in.shape[0]
    w = {
        "even": [_prep_even(ev_w_in[i], mla_w_uq[i], mla_w_ukv[i]) for i in range(n_even)],
        "hy_conv_w": hy_conv_w, "hy_conv_b": hy_conv_b, "hy_skip": hy_skip,
        "mla_q_norm": mla_q_norm, "mla_kv_norm": mla_kv_norm,
        "ev_w_out": ev_w_out.astype(BF16),
        "od_w_in": od_w_in.astype(BF16), "od_b_in": od_b_in,
        "cf_dw_w": cf_dw_w, "cf_dw_b": cf_dw_b, "cf_ln_g": cf_ln_g, "cf_ln_b": cf_ln_b,
        "pool_w": pool_w.astype(BF16), "pool_scale": pool_scale,
        "od_w_out": od_w_out.astype(BF16), "od_b_out": od_b_out,
        "ln1_g": ln1_g, "ln1_b": ln1_b, "ln2_g": ln2_g, "ln2_b": ln2_b,
        "mlp_w1": mlp_w1.astype(BF16), "mlp_w2": mlp_w2.astype(BF16),
    }
    filt = {}
    for L in sorted({x_prompt.shape[1], x_sample.shape[1]}):
        for i in range(n_even):
            g = _hy_filter(L, hy_filt_w1[i], hy_filt_b1[i], hy_filt_w_inner[i], hy_filt_b_inner[i],
                           hy_filt_freq[i], hy_filt_w_out[i])
            filt[(L, i)] = _hy_spectrum(g, L)
    return (_trunk(x_prompt, w, filt), _trunk(x_sample, w, filt))
```

```python
import functools
import math

import numpy as np
import jax
import jax.numpy as jnp
from jax import lax
from jax.experimental import pallas as pl
from jax.experimental.pallas import tpu as pltpu

F32 = jnp.float32
BF16 = jnp.bfloat16

D_MODEL = 1024
DEPTH = 4
HY_WIDTH = D_MODEL // 2
HY_EMB = 33
HY_BANDS = (HY_EMB - 1) // 2
HY_FILT = 64
HY_INNER = 2
HY_MAX_DECAY = math.log(1e-2) / 0.3
HY_MIN_DECAY = math.log(1e-2) / 1.5
MLA_HEADS = 8
MLA_NOPE = 64
MLA_ROPE = 32
MLA_V = 64
MLA_Q_RANK = 384
MLA_KV_RANK = 256
ROPE_THETA = 10000.0
CF_WIDTH = D_MODEL // 2
CF_KERNEL = 31
POOL_WIDTH = D_MODEL // 2
POOL_WINDOWS = (2, 4, 8, 16)
POOL_GROUP = POOL_WIDTH // len(POOL_WINDOWS)
D_FF = 4 * D_MODEL
DN_ALPHA = (2 * DEPTH) ** 0.25
LN_EPS = 1e-5
RMS_EPS = 1e-6

LANES = 128
SUBLANES = 8
VMEM_LIMIT = 52 * 1024 * 1024
HEAD_PAD = 128
VT_ROWS = 80
EVEN_COLS = 3 * HY_WIDTH + 6 * LANES
FFT_N2 = 32
FFT_PITCH = FFT_N2 + SUBLANES
FFT_MAX_UNROLL = 18
COPY_UNROLL = 8
HALO = 16


def _params(*sem):
    return pltpu.CompilerParams(dimension_semantics=sem, vmem_limit_bytes=VMEM_LIMIT)


def _layer_norm(r, g, b):
    mu = jnp.mean(r, axis=-1, keepdims=True)
    d = r - mu
    var = jnp.mean(d * d, axis=-1, keepdims=True)
    return d * lax.rsqrt(var + LN_EPS) * g + b


def _bdot(a, b):
    return jnp.dot(a.astype(BF16), b.astype(BF16), preferred_element_type=F32)


def _layer_tail_body(m1_ref, m2_ref, x_ref, wo1_ref, wo2_ref, bo_ref, g1_ref, b1_ref,
                     w1_ref, w2_ref, g2_ref, b2_ref, *rest, ff_chunk, streams):
    o_ref = rest[-2] if len(rest) == 4 else rest[0]
    rows = x_ref.shape[0] // streams
    sls = [slice(s * rows, (s + 1) * rows) for s in range(streams)]
    x1 = []
    for sl in sls:
        y = (jnp.dot(m1_ref[sl, :], wo1_ref[...], preferred_element_type=F32)
             + jnp.dot(m2_ref[sl, :], wo2_ref[...], preferred_element_type=F32) + bo_ref[...])
        x1.append(_layer_norm(DN_ALPHA * x_ref[sl, :] + y, g1_ref[...], b1_ref[...]))
    for sl, xs in zip(sls, x1):
        xb = xs.astype(BF16)
        acc = DN_ALPHA * xs
        for c in range(D_FF // ff_chunk):
            h = jnp.dot(xb, w1_ref[:, c * ff_chunk:(c + 1) * ff_chunk], preferred_element_type=F32)
            h = jnp.maximum(h, 0.0)
            acc = acc + jnp.dot((h * h).astype(BF16), w2_ref[c * ff_chunk:(c + 1) * ff_chunk, :],
                                preferred_element_type=F32)
        out = _layer_norm(acc, g2_ref[...], b2_ref[...])
        o_ref[sl, :] = out
        if len(rest) == 4:
            wn_ref, bn_ref, _, u_ref = rest
            u_ref[sl, :] = _bdot(out, wn_ref[...]) + bn_ref[...]


def _layer_tail(m1, m2, x, wo1, wo2, bo, g1, b1, w1, w2, g2, b2, w_next=None, b_next=None,
                tm=512, ff_chunk=1024, streams=2):
    t, d = x.shape
    k = m1.shape[1]
    row = lambda i: (i, 0)
    fix = lambda i: (0, 0)
    resident = lambda shape: pl.BlockSpec(shape, fix, pipeline_mode=pl.Buffered(1))
    vec = pl.BlockSpec((1, d), fix)
    in_specs = [pl.BlockSpec((tm, k), row), pl.BlockSpec((tm, k), row), pl.BlockSpec((tm, d), row),
                resident((k, d)), resident((k, d)), vec, vec, vec,
                resident((d, D_FF)), resident((D_FF, d)), vec, vec]
    out_specs = [pl.BlockSpec((tm, d), row)]
    out_shape = [jax.ShapeDtypeStruct((t, d), F32)]
    args = [m1, m2, x, wo1, wo2, bo, g1, b1, w1, w2, g2, b2]
    if w_next is not None:
        n = w_next.shape[1]
        in_specs += [resident((d, n)), pl.BlockSpec((1, n), fix)]
        out_specs.append(pl.BlockSpec((tm, n), row))
        out_shape.append(jax.ShapeDtypeStruct((t, n), F32))
        args += [w_next, b_next]
    outs = pl.pallas_call(
        functools.partial(_layer_tail_body, ff_chunk=ff_chunk, streams=streams),
        grid=(t // tm,),
        in_specs=in_specs,
        out_specs=out_specs,
        out_shape=out_shape,
        compiler_params=_params("parallel"),
        name="layer_tail",
    )(*args)
    return outs if w_next is not None else outs[0]


def _halo_specs(tl, L):
    per = tl // HALO
    nblk = L // HALO

    def main(b, i):
        return (b, i, 0)

    def left(b, i):
        return (b, jnp.maximum(i * per - 1, 0), 0)

    def right(b, i):
        return (b, jnp.minimum((i + 1) * per, nblk - 1), 0)

    return main, left, right


def _even_in_body(xm_ref, xl_ref, xr_ref, w_ref, cw_ref, cb_ref, z_ref, x0_ref, pm_ref, ext_ref, *, tl):
    i = pl.program_id(1)
    last = pl.num_programs(1) - 1
    hw = 3 * HY_WIDTH
    xm = xm_ref[0].astype(BF16)
    x_ext = jnp.concatenate([xl_ref[0].astype(BF16), xm, xr_ref[0].astype(BF16)], axis=0)
    e = lax.broadcasted_iota(jnp.int32, (tl + 2 * HALO, 1), 0)
    outside = ((e < HALO) & (i == 0)) | ((e >= HALO + tl) & (i == last))

    def project(part):
        cols = slice(part * HY_WIDTH, (part + 1) * HY_WIDTH)
        ext_ref[:, cols] = jnp.where(outside, 0.0, _bdot(x_ext, w_ref[:, cols]))

    def conv(part):
        cols = slice(part * HY_WIDTH, (part + 1) * HY_WIDTH)
        return (cw_ref[0:1, cols] * ext_ref[HALO - 1:HALO - 1 + tl, cols]
                + cw_ref[1:2, cols] * ext_ref[HALO:HALO + tl, cols]
                + cw_ref[2:3, cols] * ext_ref[HALO + 1:HALO + 1 + tl, cols] + cb_ref[:, cols])

    project(0)
    project(1)
    x0_ref[0] = conv(0)
    project(2)
    x1c = conv(1)
    pm_ref[0] = _bdot(xm, w_ref[:, hw:])
    z_ref[0] = x1c * conv(2)


def _even_in(x, w_in, conv_w, conv_b, tl=512):
    bsz, L, d = x.shape
    hw = 3 * HY_WIDTH
    rest = EVEN_COLS - hw
    main, left, right = _halo_specs(tl, L)
    fix = lambda b, i: (0, 0)
    hy_spec = pl.BlockSpec((1, tl, HY_WIDTH), main)
    return pl.pallas_call(
        functools.partial(_even_in_body, tl=tl),
        grid=(bsz, L // tl),
        in_specs=[pl.BlockSpec((1, tl, d), main), pl.BlockSpec((1, HALO, d), left),
                  pl.BlockSpec((1, HALO, d), right),
                  pl.BlockSpec((d, EVEN_COLS), fix), pl.BlockSpec((3, hw), fix), pl.BlockSpec((1, hw), fix)],
        out_specs=[hy_spec, hy_spec, pl.BlockSpec((1, tl, rest), main)],
        out_shape=[jax.ShapeDtypeStruct((bsz, L, HY_WIDTH), F32)] * 2
        + [jax.ShapeDtypeStruct((bsz, L, rest), F32)],
        scratch_shapes=[pltpu.VMEM((tl + 2 * HALO, hw), F32)],
        compiler_params=_params("parallel", "parallel"),
        name="even_in",
    )(x, x, x, w_in, conv_w, conv_b)


def _hy_filter_body(wt_ref, wc_ref, ws_ref, b1_ref, wi_ref, bi_ref, fr_ref, wo_ref, g_ref, *, L, tr):
    hi = lax.Precision.HIGHEST
    dot = functools.partial(jnp.dot, precision=hi, preferred_element_type=F32)
    scale_t = 1.0 / (L - 1)
    m_row = pl.program_id(0) * tr + lax.broadcasted_iota(jnp.int32, (1, tr), 1)
    j_row = jnp.where(m_row < L, m_row, 2 * L - m_row).astype(F32)
    band_step = (HY_BANDS - 1 - 1e-4) / (HY_BANDS - 1)
    bands = 1e-4 + band_step * lax.broadcasted_iota(jnp.int32, (HY_BANDS, 1), 0).astype(F32)
    arg = bands * (j_row * (2.0 * math.pi / L))
    freq = fr_ref[...]
    pre = (wt_ref[...] * (j_row * scale_t) + dot(wc_ref[...], jnp.cos(arg))
           - dot(ws_ref[...], jnp.sin(arg)) + b1_ref[...])
    h = jnp.sin(freq * pre)
    for l in range(HY_INNER):
        h = jnp.sin(freq * (dot(wi_ref[l], h) + bi_ref[l]))
    k = dot(h.T, wo_ref[...])
    m = pl.program_id(0) * tr + lax.broadcasted_iota(jnp.int32, (tr, 1), 0)
    t = jnp.where(m < L, m, 2 * L - m).astype(F32) * scale_t
    ch = lax.broadcasted_iota(jnp.int32, (1, HY_WIDTH), 1).astype(F32)
    deltas = jnp.abs(HY_MIN_DECAY + ch * ((HY_MAX_DECAY - HY_MIN_DECAY) / (HY_WIDTH - 1)))
    g_ref[...] = jnp.where(m == L, 0.0, k * jnp.exp(-t * deltas))


def _hy_filter(L, w1, b1, w_inner, b_inner, freq, w_out, tr=512):
    n = 2 * L
    full = lambda *shape: pl.BlockSpec(shape, lambda i: (0,) * len(shape))
    col = lambda a: a.reshape(-1, 1)
    w1t = w1.T
    return pl.pallas_call(
        functools.partial(_hy_filter_body, L=L, tr=tr),
        grid=(n // tr,),
        in_specs=[full(HY_FILT, 1), full(HY_FILT, HY_BANDS), full(HY_FILT, HY_BANDS), full(HY_FILT, 1),
                  full(HY_INNER, HY_FILT, HY_FILT), full(HY_INNER, HY_FILT, 1), full(HY_FILT, 1),
                  pl.BlockSpec((HY_FILT, HY_WIDTH), lambda i: (0, i // (L // tr)))],
        out_specs=pl.BlockSpec((tr, HY_WIDTH), lambda i: (i, 0)),
        out_shape=jax.ShapeDtypeStruct((n, HY_WIDTH), F32),
        compiler_params=_params("parallel"),
        name="hy_filter",
    )(w1t[:, 0:1], w1t[:, 1:1 + HY_BANDS], w1t[:, 1 + HY_BANDS:], col(b1),
      jnp.swapaxes(w_inner, 1, 2), b_inner.reshape(HY_INNER, HY_FILT, 1), col(freq), w_out)


def _fft_dims(L):
    n = 2 * L
    n1 = n // FFT_N2
    return n, n1, n1 // 2 + SUBLANES, n1 // 2


@functools.lru_cache(maxsize=None)
def _fft_tables(L):
    n, n1, kh, h1 = _fft_dims(L)
    n2 = FFT_N2
    k1 = np.arange(kh, dtype=np.float64)[:, None]
    t1 = np.arange(n1, dtype=np.float64)[None, :]
    th = 2.0 * np.pi * k1 * t1 / n1
    fy = np.concatenate([np.cos(th), -np.sin(th)], axis=0)
    k2 = np.arange(n2, dtype=np.float64)[None, :, None]
    t2 = np.arange(n2, dtype=np.float64)[None, None, :]
    ph = 2.0 * np.pi * (k2 * t2 / n2 + k1[:, :, None] * t2 / n)
    mr, mi = np.cos(ph), -np.sin(ph)
    mx = np.concatenate([np.concatenate([mr, -mi], axis=2),
                         np.concatenate([mi, mr], axis=2)], axis=1)
    mxi = np.transpose(mx, (0, 2, 1))
    wk = np.zeros((kh,), np.float64)
    wk[:n1 // 2 + 1] = 2.0
    wk[0] = 1.0
    wk[n1 // 2] = 1.0
    thi = th[:, :h1].T
    fyi = np.concatenate([np.cos(thi) * wk[None, :], -np.sin(thi) * wk[None, :]], axis=1)
    return fy, mx, mxi, fyi


def _fft_unroll(kh):
    return max(u for u in range(1, FFT_MAX_UNROLL + 1) if kh % u == 0)


def _lane_cat(parts):
    return parts[0] if len(parts) == 1 else jnp.concatenate(parts, axis=1)


def _pitch_rows(load, dst_ref, nblk, nb):
    def body(t1, carry):
        s = pl.multiple_of(t1 * FFT_N2, SUBLANES)
        d = pl.multiple_of(t1 * FFT_PITCH, SUBLANES)
        for e in range(nb):
            dst_ref[e, pl.ds(d, FFT_N2), :] = load(e, s)
        return carry
    lax.fori_loop(0, nblk, body, 0, unroll=COPY_UNROLL)


def _fft_stage_y(src_ref, fy_ref, ar_ref, ai_ref, nblk, kh, nb):
    fy = fy_ref[...]
    for t2 in range(FFT_N2):
        x = _lane_cat([src_ref[e, pl.ds(t2, nblk, stride=FFT_PITCH), :] for e in range(nb)]).astype(BF16)
        a = jnp.dot(fy, x, preferred_element_type=F32)
        for e in range(nb):
            ar_ref[e, pl.ds(t2, kh, stride=FFT_PITCH), :] = a[:kh, e * LANES:(e + 1) * LANES]
            ai_ref[e, pl.ds(t2, kh, stride=FFT_PITCH), :] = a[kh:, e * LANES:(e + 1) * LANES]


def _hy_spectrum_body(g_ref, fy_ref, mx_ref, gr_ref, gi_ref, gs_ref, ar_ref, ai_ref, *, L):
    n, n1, kh, _ = _fft_dims(L)
    _pitch_rows(lambda e, s: g_ref[pl.ds(s, FFT_N2), :], gs_ref, n1, 1)
    _fft_stage_y(gs_ref, fy_ref, ar_ref, ai_ref, n1, kh, 1)
    unroll = _fft_unroll(kh)

    def body(it, carry):
        k1s = [it * unroll + u for u in range(unroll)]
        a = []
        for k1 in k1s:
            src = pl.multiple_of(k1 * FFT_PITCH, SUBLANES)
            a.append(jnp.concatenate([ar_ref[0, pl.ds(src, FFT_N2), :], ai_ref[0, pl.ds(src, FFT_N2), :]],
                                     axis=0).astype(BF16))
        z = [jnp.dot(mx_ref[k1], x, preferred_element_type=F32) * (1.0 / n) for k1, x in zip(k1s, a)]
        for k1, zz in zip(k1s, z):
            dst = pl.multiple_of(k1 * FFT_N2, SUBLANES)
            gr_ref[pl.ds(dst, FFT_N2), :] = zz[:FFT_N2]
            gi_ref[pl.ds(dst, FFT_N2), :] = zz[FFT_N2:]
        return carry
    lax.fori_loop(0, kh // unroll, body, 0)


def _hy_spectrum(g, L):
    n, n1, kh, _ = _fft_dims(L)
    fy, mx, _, _ = _fft_tables(L)
    out_spec = pl.BlockSpec((kh * FFT_N2, LANES), lambda c: (0, c))
    return pl.pallas_call(
        functools.partial(_hy_spectrum_body, L=L),
        grid=(HY_WIDTH // LANES,),
        in_specs=[pl.BlockSpec((n, LANES), lambda c: (0, c)),
                  pl.BlockSpec((2 * kh, n1), lambda c: (0, 0)),
                  pl.BlockSpec((kh, 2 * FFT_N2, 2 * FFT_N2), lambda c: (0, 0, 0))],
        out_specs=[out_spec, out_spec],
        out_shape=[jax.ShapeDtypeStruct((kh * FFT_N2, HY_WIDTH), F32)] * 2,
        scratch_shapes=[pltpu.VMEM((1, n1 * FFT_PITCH, LANES), F32),
                        pltpu.VMEM((1, kh * FFT_PITCH, LANES), F32),
                        pltpu.VMEM((1, kh * FFT_PITCH, LANES), F32)],
        compiler_params=_params("parallel"),
        name="hy_spectrum",
    )(g, jnp.asarray(fy, BF16), jnp.asarray(mx, BF16))


def _hy_conv_body(z_ref, x0_ref, gr_ref, gi_ref, fy_ref, mx_ref, mxi_ref, fyi_ref, skip_ref, o_ref,
                  zs_ref, ar_ref, ai_ref, *, L, nb):
    n, n1, kh, h1 = _fft_dims(L)
    seqs = range(nb)
    _pitch_rows(lambda e, s: z_ref[e, pl.ds(s, FFT_N2), :], zs_ref, h1, nb)
    _fft_stage_y(zs_ref, fy_ref, ar_ref, ai_ref, h1, kh, nb)

    unroll = _fft_unroll(kh)

    def spectral(it, carry):
        k1s = [it * unroll + u for u in range(unroll)]
        rows = [pl.multiple_of(k1 * FFT_PITCH, SUBLANES) for k1 in k1s]
        grows = [pl.multiple_of(k1 * FFT_N2, SUBLANES) for k1 in k1s]
        a = [_lane_cat([jnp.concatenate([ar_ref[e, pl.ds(r, FFT_N2), :], ai_ref[e, pl.ds(r, FFT_N2), :]], axis=0)
                        for e in seqs]).astype(BF16) for r in rows]
        z = [jnp.dot(mx_ref[k1], x, preferred_element_type=F32) for k1, x in zip(k1s, a)]
        y = []
        for zz, g in zip(z, grows):
            zr, zi = zz[:FFT_N2], zz[FFT_N2:]
            gr = _lane_cat([gr_ref[pl.ds(g, FFT_N2), :]] * nb)
            gi = _lane_cat([gi_ref[pl.ds(g, FFT_N2), :]] * nb)
            y.append(jnp.concatenate([zr * gr - zi * gi, zr * gi + zi * gr], axis=0).astype(BF16))
        b = [jnp.dot(mxi_ref[k1], x, preferred_element_type=F32) for k1, x in zip(k1s, y)]
        for r, bb in zip(rows, b):
            for e in seqs:
                ar_ref[e, pl.ds(r, FFT_N2), :] = bb[:FFT_N2, e * LANES:(e + 1) * LANES]
                ai_ref[e, pl.ds(r, FFT_N2), :] = bb[FFT_N2:, e * LANES:(e + 1) * LANES]
        return carry
    lax.fori_loop(0, kh // unroll, spectral, 0)

    fyi = fyi_ref[...]
    for t2 in range(FFT_N2):
        b = _lane_cat([jnp.concatenate([ar_ref[e, pl.ds(t2, kh, stride=FFT_PITCH), :],
                                        ai_ref[e, pl.ds(t2, kh, stride=FFT_PITCH), :]], axis=0) for e in seqs])
        y = jnp.dot(fyi, b.astype(BF16), preferred_element_type=F32)
        for e in seqs:
            zs_ref[e, pl.ds(t2, h1, stride=FFT_PITCH), :] = y[:, e * LANES:(e + 1) * LANES]

    skip = skip_ref[...]

    def finish(t1, carry):
        s = pl.multiple_of(t1 * FFT_N2, SUBLANES)
        d = pl.multiple_of(t1 * FFT_PITCH, SUBLANES)
        for e in seqs:
            y = zs_ref[e, pl.ds(d, FFT_N2), :] + z_ref[e, pl.ds(s, FFT_N2), :] * skip
            o_ref[e, pl.ds(s, FFT_N2), :] = (x0_ref[e, pl.ds(s, FFT_N2), :] * y).astype(o_ref.dtype)
        return carry
    lax.fori_loop(0, h1, finish, 0, unroll=COPY_UNROLL)


def _hy_conv(z, x0c, gr, gi, skip, L, nb=2):
    bsz = z.shape[0]
    n, n1, kh, h1 = _fft_dims(L)
    fy, mx, mxi, fyi = _fft_tables(L)
    seq = pl.BlockSpec((nb, L, LANES), lambda c, b: (b, 0, c))
    fixed = lambda shape, imap: pl.BlockSpec(shape, imap, pipeline_mode=pl.Buffered(1))
    spec = fixed((kh * FFT_N2, LANES), lambda c, b: (0, c))
    mat = fixed((kh, 2 * FFT_N2, 2 * FFT_N2), lambda c, b: (0, 0, 0))
    return pl.pallas_call(
        functools.partial(_hy_conv_body, L=L, nb=nb),
        grid=(HY_WIDTH // LANES, bsz // nb),
        in_specs=[seq, seq, spec, spec,
                  pl.BlockSpec((2 * kh, h1), lambda c, b: (0, 0)),
                  mat, mat,
                  pl.BlockSpec((h1, 2 * kh), lambda c, b: (0, 0)),
                  pl.BlockSpec((1, LANES), lambda c, b: (0, c))],
        out_specs=seq,
        out_shape=jax.ShapeDtypeStruct((bsz, L, HY_WIDTH), BF16),
        scratch_shapes=[pltpu.VMEM((nb, h1 * FFT_PITCH, LANES), F32),
                        pltpu.VMEM((nb, kh * FFT_PITCH, LANES), F32),
                        pltpu.VMEM((nb, kh * FFT_PITCH, LANES), F32)],
        compiler_params=_params("parallel", "parallel"),
        name="hy_conv",
    )(z, x0c, gr, gi, jnp.asarray(fy[:, :h1], BF16), jnp.asarray(mx, BF16), jnp.asarray(mxi, BF16),
      jnp.asarray(fyi, BF16), skip.reshape(1, -1))


def _rms(x, g):
    return x * lax.rsqrt(jnp.mean(x * x, axis=-1, keepdims=True) + RMS_EPS) * g


def _mla_proj_body(p_ref, qn_ref, wq_ref, kvn_ref, wk_ref, wvt_ref,
                   qc_ref, qlo_ref, qhi_ref, kt_ref, q_ref, k_ref, vt_ref):
    p = p_ref[...]
    cq = _rms(p[:, :MLA_Q_RANK], qn_ref[...]).astype(BF16)
    ckv = _rms(p[:, MLA_Q_RANK:MLA_Q_RANK + MLA_KV_RANK], kvn_ref[...]).astype(BF16)
    q = jnp.dot(cq, wq_ref[...], preferred_element_type=F32)
    k = jnp.dot(ckv, wk_ref[...], preferred_element_type=F32)
    kr = p[:, MLA_Q_RANK + MLA_KV_RANK:] * kt_ref[...]
    kr = kr + pltpu.roll(kr, LANES - MLA_ROPE, axis=1)
    lane = lax.broadcasted_iota(jnp.int32, (1, HEAD_PAD), 1)
    in_rope = (lane >= MLA_NOPE) & (lane < MLA_NOPE + MLA_ROPE)
    kr = jnp.where(in_rope, pltpu.roll(kr, MLA_NOPE, axis=1), 0.0)
    qc = qc_ref[...]
    qlo = qlo_ref[...]
    qhi = qhi_ref[...]
    half = MLA_ROPE // 2
    for h in range(MLA_HEADS):
        sl = slice(h * HEAD_PAD, (h + 1) * HEAD_PAD)
        x = q[:, sl]
        q_ref[:, sl] = (x * qc + pltpu.roll(x, HEAD_PAD - half, axis=1) * qlo
                        + pltpu.roll(x, half, axis=1) * qhi).astype(BF16)
        k_ref[:, sl] = (k[:, sl] + kr).astype(BF16)
    vt = lax.dot_general(wvt_ref[...], ckv, (((1,), (1,)), ((), ())), preferred_element_type=F32)
    row = lax.broadcasted_iota(jnp.int32, (MLA_HEADS * VT_ROWS, 1), 0) % VT_ROWS
    vt_ref[0] = (vt + jnp.where(row == MLA_V, 1.0, 0.0)).astype(BF16)


def _mla_proj(p, L, q_norm, wq, kv_norm, wk, wvt, qc, qlo, qhi, kt, tm=512):
    t = p.shape[0]
    hw = MLA_HEADS * HEAD_PAD
    nper = L // tm
    fix = lambda i: (0, 0)
    pos = lambda i: (i % nper, 0)
    row = lambda i: (i, 0)
    wide = 6 * LANES
    return pl.pallas_call(
        _mla_proj_body,
        grid=(t // tm,),
        in_specs=[pl.BlockSpec((tm, wide), row),
                  pl.BlockSpec((1, MLA_Q_RANK), fix),
                  pl.BlockSpec((MLA_Q_RANK, hw), fix),
                  pl.BlockSpec((1, MLA_KV_RANK), fix),
                  pl.BlockSpec((MLA_KV_RANK, hw), fix),
                  pl.BlockSpec((MLA_HEADS * VT_ROWS, MLA_KV_RANK), fix),
                  pl.BlockSpec((tm, HEAD_PAD), pos), pl.BlockSpec((tm, HEAD_PAD), pos),
                  pl.BlockSpec((tm, HEAD_PAD), pos), pl.BlockSpec((tm, LANES), pos)],
        out_specs=[pl.BlockSpec((tm, hw), row), pl.BlockSpec((tm, hw), row),
                   pl.BlockSpec((1, MLA_HEADS * VT_ROWS, tm), lambda i: (i // nper, 0, i % nper))],
        out_shape=[jax.ShapeDtypeStruct((t, hw), BF16), jax.ShapeDtypeStruct((t, hw), BF16),
                   jax.ShapeDtypeStruct((t // L, MLA_HEADS * VT_ROWS, L), BF16)],
        compiler_params=_params("parallel"),
        name="mla_proj",
    )(p, q_norm, wq, kv_norm, wk, wvt, qc, qlo, qhi, kt)


def _attn_body(q_ref, k_ref, vt_ref, o_ref, *, kc):
    L = k_ref.shape[1]
    nchunk = L // kc
    heads = range(2)
    qs = [q_ref[0, :, h * HEAD_PAD:(h + 1) * HEAD_PAD] for h in heads]

    def scores(h, c):
        return lax.dot_general(k_ref[0, c * kc:(c + 1) * kc, h * HEAD_PAD:(h + 1) * HEAD_PAD], qs[h],
                               (((1,), (1,)), ((), ())), preferred_element_type=F32)

    m = [None, None]
    acc = [None, None]
    s_next = [scores(h, 0) for h in heads]
    for c in range(nchunk):
        s_cur = s_next
        if c + 1 < nchunk:
            s_next = [scores(h, c + 1) for h in heads]
        for h in heads:
            s = s_cur[h]
            cmax = jnp.max(s, axis=0, keepdims=True)
            m_new = cmax if c == 0 else jnp.maximum(m[h], cmax)
            e = jnp.exp2(s - m_new).astype(BF16)
            pv = jnp.dot(vt_ref[0, h * VT_ROWS:(h + 1) * VT_ROWS, c * kc:(c + 1) * kc], e,
                         preferred_element_type=F32)
            acc[h] = pv if c == 0 else acc[h] * jnp.exp2(m[h] - m_new) + pv
            m[h] = m_new
    out_t = jnp.concatenate([acc[h][:MLA_V] / acc[h][MLA_V:MLA_V + 1] for h in heads], axis=0)
    o_ref[0] = out_t.T.astype(o_ref.dtype)


def _attention(q, k, vt, tq=1024, kc=256):
    bsz, L, _ = q.shape
    return pl.pallas_call(
        functools.partial(_attn_body, kc=kc),
        grid=(bsz, MLA_HEADS // 2, L // tq),
        in_specs=[pl.BlockSpec((1, tq, 2 * HEAD_PAD), lambda b, g, i: (b, i, g)),
                  pl.BlockSpec((1, L, 2 * HEAD_PAD), lambda b, g, i: (b, 0, g)),
                  pl.BlockSpec((1, 2 * VT_ROWS, L), lambda b, g, i: (b, g, 0))],
        out_specs=pl.BlockSpec((1, tq, 2 * MLA_V), lambda b, g, i: (b, i, g)),
        out_shape=jax.ShapeDtypeStruct((bsz, L, MLA_HEADS * MLA_V), BF16),
        compiler_params=_params("parallel", "parallel", "parallel"),
        name="attention",
    )(q, k, vt)


def _odd_mix_body(um_ref, ul_ref, ur_ref, dw_ref, db_ref, lg_ref, lb_ref, pw_ref, ps_ref,
                  cf_ref, pool_ref, h_ref, u_ref, conv_ref, hp_ref, sa_ref, sb_ref, *, tl, L, rows):
    i = pl.program_id(1)
    last = pl.num_programs(1) - 1

    def glu(u):
        return u[:, :CF_WIDTH] * jax.nn.sigmoid(u[:, CF_WIDTH:2 * CF_WIDTH])

    ul = ul_ref[0]
    um = um_ref[0]
    ur = ur_ref[0]
    h_ref[0:HALO, :] = jnp.where(i == 0, 0.0, glu(ul))
    h_ref[HALO:HALO + tl, :] = glu(um)
    h_ref[HALO + tl:, :] = jnp.where(i == last, 0.0, glu(ur))
    u_ref[0:HALO, :] = jnp.where(i == 0, 0.0, ul[:, 2 * CF_WIDTH:])
    u_ref[HALO:HALO + tl, :] = um[:, 2 * CF_WIDTH:]
    u_ref[HALO + tl:, :] = jnp.where(i == last, 0.0, ur[:, 2 * CF_WIDTH:])

    span = tl + 2 * HALO - SUBLANES
    for j in range(SUBLANES):
        hp_ref[j, 0:span, :] = h_ref[j:j + span, :]

    off = HALO - CF_KERNEL // 2
    for r in range(0, tl, rows):
        for c in range(0, CF_WIDTH, LANES):
            acc = jnp.zeros((rows, LANES), F32) + db_ref[:, c:c + LANES]
            for k in range(CF_KERNEL):
                j = (off + k) % SUBLANES
                base = r + off + k - j
                acc = acc + dw_ref[k:k + 1, c:c + LANES] * hp_ref[j, base:base + rows, c:c + LANES]
            conv_ref[r:r + rows, c:c + LANES] = acc
    hn = _layer_norm(conv_ref[...], lg_ref[...], lb_ref[...])
    cf_ref[0] = (hn * jax.nn.sigmoid(hn)).astype(cf_ref.dtype)

    pos = i * tl + lax.broadcasted_iota(jnp.int32, (tl, 1), 0)
    ext = tl + 2 * HALO
    first, stop = SUBLANES, ext - SUBLANES
    for buf in (sa_ref, sb_ref):
        buf[stop:ext, :] = jnp.zeros((SUBLANES, POOL_GROUP), F32)

    def window_sum(w, cols):
        lo = w // 2
        src = lambda a, b: u_ref[a:b, cols]
        width, bufs = 1, [sa_ref, sb_ref]
        while 2 * width < w:
            dst = bufs[0]
            dst[first:stop, :] = src(first, stop) + src(first + width, stop + width)
            src = functools.partial(lambda d, a, b: d[a:b, :], dst)
            bufs.reverse()
            width *= 2
        a0 = HALO - lo
        return src(a0, a0 + tl) + src(a0 + width, a0 + width + tl)

    for gi, w in enumerate(POOL_WINDOWS):
        lo = w // 2
        hi = w - 1 - lo
        c0 = gi * POOL_GROUP
        tot = window_sum(w, slice(c0, c0 + POOL_GROUP))
        cnt = (jnp.minimum(pos + hi + 1, L) - jnp.maximum(pos - lo, 0)).astype(F32)
        dlt = tot / cnt - u_ref[HALO:HALO + tl, c0:c0 + POOL_GROUP]
        pool_ref[0, :, c0:c0 + POOL_GROUP] = (
            jnp.dot(dlt.astype(BF16), pw_ref[gi], preferred_element_type=F32)
            * ps_ref[:, c0:c0 + POOL_GROUP]).astype(pool_ref.dtype)


def _odd_mix(u, dw_w, dw_b, ln_g, ln_b, pool_w, pool_scale, tl=512, rows=128):
    bsz, L, w = u.shape
    main, left, right = _halo_specs(tl, L)
    fix2 = lambda b, i: (0, 0)
    out_spec = pl.BlockSpec((1, tl, CF_WIDTH), main)
    return pl.pallas_call(
        functools.partial(_odd_mix_body, tl=tl, L=L, rows=rows),
        grid=(bsz, L // tl),
        in_specs=[pl.BlockSpec((1, tl, w), main),
                  pl.BlockSpec((1, HALO, w), left),
                  pl.BlockSpec((1, HALO, w), right),
                  pl.BlockSpec((CF_KERNEL, CF_WIDTH), fix2),
                  pl.BlockSpec((1, CF_WIDTH), fix2), pl.BlockSpec((1, CF_WIDTH), fix2),
                  pl.BlockSpec((1, CF_WIDTH), fix2),
                  pl.BlockSpec((len(POOL_WINDOWS), POOL_GROUP, POOL_GROUP), lambda b, i: (0, 0, 0)),
                  pl.BlockSpec((1, POOL_WIDTH), fix2)],
        out_specs=[out_spec, out_spec],
        out_shape=[jax.ShapeDtypeStruct((bsz, L, CF_WIDTH), BF16)] * 2,
        scratch_shapes=[pltpu.VMEM((tl + 2 * HALO, CF_WIDTH), F32),
                        pltpu.VMEM((tl + 2 * HALO, POOL_WIDTH), F32),
                        pltpu.VMEM((tl, CF_WIDTH), F32),
                        pltpu.VMEM((SUBLANES, tl + 2 * HALO, CF_WIDTH), F32),
                        pltpu.VMEM((tl + 2 * HALO, POOL_GROUP), F32),
                        pltpu.VMEM((tl + 2 * HALO, POOL_GROUP), F32)],
        compiler_params=_params("parallel", "parallel"),
        name="odd_mix",
    )(u, u, u, dw_w, dw_b, ln_g, ln_b, pool_w, pool_scale)


def _rope_tables(L):
    inv = 1.0 / (ROPE_THETA ** (jnp.arange(0, MLA_ROPE, 2, dtype=F32) / MLA_ROPE))
    ang = jnp.arange(L, dtype=F32)[:, None] * inv[None, :]
    cos, sin = jnp.cos(ang), jnp.sin(ang)
    scale = (MLA_NOPE + MLA_ROPE) ** -0.5 * math.log2(math.e)
    ones = jnp.ones((L, MLA_NOPE), F32)
    zeros = jnp.zeros((L, MLA_NOPE), F32)
    zhalf = jnp.zeros_like(sin)
    pad = jnp.zeros((L, HEAD_PAD - MLA_NOPE - MLA_ROPE), F32)
    qc = scale * jnp.concatenate([ones, cos, cos, pad], axis=1)
    qlo = scale * jnp.concatenate([zeros, -sin, zhalf, pad], axis=1)
    qhi = scale * jnp.concatenate([zeros, zhalf, sin, pad], axis=1)
    kt = jnp.concatenate([cos, cos, sin, sin, jnp.zeros((L, LANES - 2 * MLA_ROPE), F32)], axis=1)
    return qc, qlo, qhi, kt


def _rot_half_cols(w):
    half = w.shape[-1] // 2
    return jnp.concatenate([-w[..., half:], w[..., :half]], axis=-1)


def _prep_even(ev_w_in, mla_w_uq, mla_w_ukv):
    k = ev_w_in.shape[0]
    base = 3 * HY_WIDTH + MLA_Q_RANK + MLA_KV_RANK
    kr = ev_w_in[:, base:base + MLA_ROPE]
    w_in = jnp.concatenate(
        [ev_w_in[:, :base], kr, _rot_half_cols(kr),
         jnp.zeros((k, LANES - 2 * MLA_ROPE), F32)], axis=1).astype(BF16)
    uq = mla_w_uq.reshape(MLA_Q_RANK, MLA_HEADS, MLA_NOPE + MLA_ROPE)
    zq = jnp.zeros((MLA_Q_RANK, MLA_HEADS, HEAD_PAD - MLA_NOPE - MLA_ROPE), F32)
    wq = jnp.concatenate([uq, zq], axis=2).reshape(MLA_Q_RANK, -1).astype(BF16)
    ukv = mla_w_ukv.reshape(MLA_KV_RANK, MLA_HEADS, MLA_NOPE + MLA_V)
    zk = jnp.zeros((MLA_KV_RANK, MLA_HEADS, HEAD_PAD - MLA_NOPE), F32)
    wk = jnp.concatenate([ukv[:, :, :MLA_NOPE], zk], axis=2).reshape(MLA_KV_RANK, -1).astype(BF16)
    zv = jnp.zeros((MLA_KV_RANK, MLA_HEADS, VT_ROWS - MLA_V), F32)
    wvt = jnp.concatenate([ukv[:, :, MLA_NOPE:], zv], axis=2).reshape(MLA_KV_RANK, -1).T.astype(BF16)
    return w_in, wq, wk, wvt


def _trunk(x, w, filt):
    bsz, L, d = x.shape
    t = bsz * L
    xf = x.reshape(t, d)
    qc, qlo, qhi, kt = _rope_tables(L)
    zero_b = jnp.zeros((1, d), F32)
    for layer in range(DEPTH):
        i = layer // 2
        if layer % 2 == 0:
            w_in, wq, wk, wvt = w["even"][i]
            z, x0c, p_mla = _even_in(xf.reshape(bsz, L, d), w_in, w["hy_conv_w"][i],
                                     w["hy_conv_b"][i].reshape(1, -1))
            gr, gi = filt[(L, i)]
            m1 = _hy_conv(z, x0c, gr, gi, w["hy_skip"][i], L).reshape(t, HY_WIDTH)
            q, k, vt = _mla_proj(p_mla.reshape(t, -1), L, w["mla_q_norm"][i].reshape(1, -1), wq,
                                 w["mla_kv_norm"][i].reshape(1, -1), wk, wvt, qc, qlo, qhi, kt)
            hw = MLA_HEADS * HEAD_PAD
            m2 = _attention(q.reshape(bsz, L, hw), k.reshape(bsz, L, hw), vt).reshape(t, MLA_HEADS * MLA_V)
            w_out = w["ev_w_out"][i]
            b_out = zero_b
        else:
            cf, pool = _odd_mix(u.reshape(bsz, L, -1), w["cf_dw_w"][i], w["cf_dw_b"][i].reshape(1, -1),
                                w["cf_ln_g"][i].reshape(1, -1), w["cf_ln_b"][i].reshape(1, -1),
                                w["pool_w"][i], w["pool_scale"][i].reshape(1, -1))
            m1 = cf.reshape(t, CF_WIDTH)
            m2 = pool.reshape(t, POOL_WIDTH)
            w_out = w["od_w_out"][i]
            b_out = w["od_b_out"][i].reshape(1, -1)
        half = w_out.shape[0] // 2
        tail_args = (m1, m2, xf, w_out[:half], w_out[half:], b_out,
                     w["ln1_g"][layer].reshape(1, -1), w["ln1_b"][layer].reshape(1, -1),
                     w["mlp_w1"][layer], w["mlp_w2"][layer],
                     w["ln2_g"][layer].reshape(1, -1), w["ln2_b"][layer].reshape(1, -1))
        if layer % 2 == 0 and layer + 1 < DEPTH:
            xf, u = _layer_tail(*tail_args, w["od_w_in"][i], w["od_b_in"][i].reshape(1, -1))
        else:
            xf = _layer_tail(*tail_args)
    return xf.reshape(bsz, L, d)


def kernel(x_prompt, x_sample, ev_w_in, hy_conv_w, hy_conv_b, hy_filt_w1, hy_filt_b1, hy_filt_w_inner, hy_filt_b_inner, hy_filt_freq, hy_filt_w_out, hy_skip, mla_q_norm, mla_w_uq, mla_kv_norm, mla_w_ukv, ev_w_out, od_w_in, od_b_in, cf_dw_w, cf_dw_b, cf_ln_g, cf_ln_b, pool_w, pool_scale, od_w_out, od_b_out, ln1_g, ln1_b, mlp_w1, mlp_w2, ln2_g, ln2_b):
    n_even = ev_w_in.shape[0]
    w = {
        "even": [_prep_even(ev_w_in[i], mla_w_uq[i], mla_w_ukv[i]) for i in range(n_even)],
        "hy_conv_w": hy_conv_w, "hy_conv_b": hy_conv_b, "hy_skip": hy_skip,
        "mla_q_norm": mla_q_norm, "mla_kv_norm": mla_kv_norm,
        "ev_w_out": ev_w_out.astype(BF16),
        "od_w_in": od_w_in.astype(BF16), "od_b_in": od_b_in,
        "cf_dw_w": cf_dw_w, "cf_dw_b": cf_dw_b, "cf_ln_g": cf_ln_g, "cf_ln_b": cf_ln_b,
        "pool_w": pool_w.astype(BF16), "pool_scale": pool_scale,
        "od_w_out": od_w_out.astype(BF16), "od_b_out": od_b_out,
        "ln1_g": ln1_g, "ln1_b": ln1_b, "ln2_g": ln2_g, "ln2_b": ln2_b,
        "mlp_w1": mlp_w1.astype(BF16), "mlp_w2": mlp_w2.astype(BF16),
    }
    filt = {}
    for L in sorted({x_prompt.shape[1], x_sample.shape[1]}):
        for i in range(n_even):
            g = _hy_filter(L, hy_filt_w1[i], hy_filt_b1[i], hy_filt_w_inner[i], hy_filt_b_inner[i],
                           hy_filt_freq[i], hy_filt_w_out[i])
            filt[(L, i)] = _hy_spectrum(g, L)
    return (_trunk(x_prompt, w, filt), _trunk(x_sample, w, filt))
```

```python
import functools
import math

import numpy as np
import jax
import jax.numpy as jnp
from jax import lax
from jax.experimental import pallas as pl
from jax.experimental.pallas import tpu as pltpu

F32 = jnp.float32
BF16 = jnp.bfloat16

D_MODEL = 1024
DEPTH = 4
HY_WIDTH = D_MODEL // 2
HY_EMB = 33
HY_BANDS = (HY_EMB - 1) // 2
HY_FILT = 64
HY_INNER = 2
HY_MAX_DECAY = math.log(1e-2) / 0.3
HY_MIN_DECAY = math.log(1e-2) / 1.5
MLA_HEADS = 8
MLA_NOPE = 64
MLA_ROPE = 32
MLA_V = 64
MLA_Q_RANK = 384
MLA_KV_RANK = 256
ROPE_THETA = 10000.0
CF_WIDTH = D_MODEL // 2
CF_KERNEL = 31
POOL_WIDTH = D_MODEL // 2
POOL_WINDOWS = (2, 4, 8, 16)
POOL_GROUP = POOL_WIDTH // len(POOL_WINDOWS)
D_FF = 4 * D_MODEL
DN_ALPHA = (2 * DEPTH) ** 0.25
LN_EPS = 1e-5
RMS_EPS = 1e-6

LANES = 128
SUBLANES = 8
VMEM_LIMIT = 52 * 1024 * 1024
HEAD_PAD = 128
VT_ROWS = 80
EVEN_COLS = 3 * HY_WIDTH + 6 * LANES
FFT_N2 = 32
FFT_PITCH = FFT_N2 + SUBLANES
FFT_MAX_UNROLL = 18
COPY_UNROLL = 8
HALO = 16


def _params(*sem):
    return pltpu.CompilerParams(dimension_semantics=sem, vmem_limit_bytes=VMEM_LIMIT)


def _layer_norm(r, g, b):
    mu = jnp.mean(r, axis=-1, keepdims=True)
    d = r - mu
    var = jnp.mean(d * d, axis=-1, keepdims=True)
    return d * lax.rsqrt(var + LN_EPS) * g + b


def _bdot(a, b):
    return jnp.dot(a.astype(BF16), b.astype(BF16), preferred_element_type=F32)


def _layer_tail_body(m1_ref, m2_ref, x_ref, wo1_ref, wo2_ref, bo_ref, g1_ref, b1_ref,
                     w1_ref, w2_ref, g2_ref, b2_ref, *rest, ff_chunk, streams):
    o_ref = rest[-2] if len(rest) == 4 else rest[0]
    rows = x_ref.shape[0] // streams
    sls = [slice(s * rows, (s + 1) * rows) for s in range(streams)]
    x1 = []
    for sl in sls:
        y = (jnp.dot(m1_ref[sl, :], wo1_ref[...], preferred_element_type=F32)
             + jnp.dot(m2_ref[sl, :], wo2_ref[...], preferred_element_type=F32) + bo_ref[...])
        x1.append(_layer_norm(DN_ALPHA * x_ref[sl, :] + y, g1_ref[...], b1_ref[...]))
    for sl, xs in zip(sls, x1):
        xb = xs.astype(BF16)
        acc = DN_ALPHA * xs
        for c in range(D_FF // ff_chunk):
            h = jnp.dot(xb, w1_ref[:, c * ff_chunk:(c + 1) * ff_chunk], preferred_element_type=F32)
            h = jnp.maximum(h, 0.0)
            acc = acc + jnp.dot((h * h).astype(BF16), w2_ref[c * ff_chunk:(c + 1) * ff_chunk, :],
                                preferred_element_type=F32)
        out = _layer_norm(acc, g2_ref[...], b2_ref[...])
        o_ref[sl, :] = out
        if len(rest) == 4:
            wn_ref, bn_ref, _, u_ref = rest
            u_ref[sl, :] = _bdot(out, wn_ref[...]) + bn_ref[...]


def _layer_tail(m1, m2, x, wo1, wo2, bo, g1, b1, w1, w2, g2, b2, w_next=None, b_next=None,
                tm=512, ff_chunk=1024, streams=2):
    t, d = x.shape
    k = m1.shape[1]
    row = lambda i: (i, 0)
    fix = lambda i: (0, 0)
    resident = lambda shape: pl.BlockSpec(shape, fix, pipeline_mode=pl.Buffered(1))
    vec = pl.BlockSpec((1, d), fix)
    in_specs = [pl.BlockSpec((tm, k), row), pl.BlockSpec((tm, k), row), pl.BlockSpec((tm, d), row),
                resident((k, d)), resident((k, d)), vec, vec, vec,
                resident((d, D_FF)), resident((D_FF, d)), vec, vec]
    out_specs = [pl.BlockSpec((tm, d), row)]
    out_shape = [jax.ShapeDtypeStruct((t, d), F32)]
    args = [m1, m2, x, wo1, wo2, bo, g1, b1, w1, w2, g2, b2]
    if w_next is not None:
        n = w_next.shape[1]
        in_specs += [resident((d, n)), pl.BlockSpec((1, n), fix)]
        out_specs.append(pl.BlockSpec((tm, n), row))
        out_shape.append(jax.ShapeDtypeStruct((t, n), F32))
        args += [w_next, b_next]
    outs = pl.pallas_call(
        functools.partial(_layer_tail_body, ff_chunk=ff_chunk, streams=streams),
        grid=(t // tm,),
        in_specs=in_specs,
        out_specs=out_specs,
        out_shape=out_shape,
        compiler_params=_params("parallel"),
        name="layer_tail",
    )(*args)
    return outs if w_next is not None else outs[0]


def _halo_specs(tl, L):
    per = tl // HALO
    nblk = L // HALO

    def main(b, i):
        return (b, i, 0)

    def left(b, i):
        return (b, jnp.maximum(i * per - 1, 0), 0)

    def right(b, i):
        return (b, jnp.minimum((i + 1) * per, nblk - 1), 0)

    return main, left, right


def _even_in_body(xm_ref, xl_ref, xr_ref, w_ref, cw_ref, cb_ref, z_ref, x0_ref, pm_ref, ext_ref, *, tl):
    i = pl.program_id(1)
    last = pl.num_programs(1) - 1
    hw = 3 * HY_WIDTH
    xm = xm_ref[0].astype(BF16)
    x_ext = jnp.concatenate([xl_ref[0].astype(BF16), xm, xr_ref[0].astype(BF16)], axis=0)
    e = lax.broadcasted_iota(jnp.int32, (tl + 2 * HALO, 1), 0)
    outside = ((e < HALO) & (i == 0)) | ((e >= HALO + tl) & (i == last))

    def project(part):
        cols = slice(part * HY_WIDTH, (part + 1) * HY_WIDTH)
        ext_ref[:, cols] = jnp.where(outside, 0.0, _bdot(x_ext, w_ref[:, cols]))

    def conv(part):
        cols = slice(part * HY_WIDTH, (part + 1) * HY_WIDTH)
        return (cw_ref[0:1, cols] * ext_ref[HALO - 1:HALO - 1 + tl, cols]
                + cw_ref[1:2, cols] * ext_ref[HALO:HALO + tl, cols]
                + cw_ref[2:3, cols] * ext_ref[HALO + 1:HALO + 1 + tl, cols] + cb_ref[:, cols])

    project(0)
    project(1)
    x0_ref[0] = conv(0)
    project(2)
    x1c = conv(1)
    pm_ref[0] = _bdot(xm, w_ref[:, hw:])
    z_ref[0] = x1c * conv(2)


def _even_in(x, w_in, conv_w, conv_b, tl=512):
    bsz, L, d = x.shape
    hw = 3 * HY_WIDTH
    rest = EVEN_COLS - hw
    main, left, right = _halo_specs(tl, L)
    fix = lambda b, i: (0, 0)
    hy_spec = pl.BlockSpec((1, tl, HY_WIDTH), main)
    return pl.pallas_call(
        functools.partial(_even_in_body, tl=tl),
        grid=(bsz, L // tl),
        in_specs=[pl.BlockSpec((1, tl, d), main), pl.BlockSpec((1, HALO, d), left),
                  pl.BlockSpec((1, HALO, d), right),
                  pl.BlockSpec((d, EVEN_COLS), fix), pl.BlockSpec((3, hw), fix), pl.BlockSpec((1, hw), fix)],
        out_specs=[hy_spec, hy_spec, pl.BlockSpec((1, tl, rest), main)],
        out_shape=[jax.ShapeDtypeStruct((bsz, L, HY_WIDTH), F32)] * 2
        + [jax.ShapeDtypeStruct((bsz, L, rest), F32)],
        scratch_shapes=[pltpu.VMEM((tl + 2 * HALO, hw), F32)],
        compiler_params=_params("parallel", "parallel"),
        name="even_in",
    )(x, x, x, w_in, conv_w, conv_b)


def _hy_filter_body(wt_ref, wc_ref, ws_ref, b1_ref, wi_ref, bi_ref, fr_ref, wo_ref, g_ref, *, L, tr):
    hi = lax.Precision.HIGHEST
    dot = functools.partial(jnp.dot, precision=hi, preferred_element_type=F32)
    scale_t = 1.0 / (L - 1)
    m_row = pl.program_id(0) * tr + lax.broadcasted_iota(jnp.int32, (1, tr), 1)
    j_row = jnp.where(m_row < L, m_row, 2 * L - m_row).astype(F32)
    band_step = (HY_BANDS - 1 - 1e-4) / (HY_BANDS - 1)
    bands = 1e-4 + band_step * lax.broadcasted_iota(jnp.int32, (HY_BANDS, 1), 0).astype(F32)
    arg = bands * (j_row * (2.0 * math.pi / L))
    freq = fr_ref[...]
    pre = (wt_ref[...] * (j_row * scale_t) + dot(wc_ref[...], jnp.cos(arg))
           - dot(ws_ref[...], jnp.sin(arg)) + b1_ref[...])
    h = jnp.sin(freq * pre)
    for l in range(HY_INNER):
        h = jnp.sin(freq * (dot(wi_ref[l], h) + bi_ref[l]))
    k = dot(h.T, wo_ref[...])
    m = pl.program_id(0) * tr + lax.broadcasted_iota(jnp.int32, (tr, 1), 0)
    t = jnp.where(m < L, m, 2 * L - m).astype(F32) * scale_t
    ch = lax.broadcasted_iota(jnp.int32, (1, HY_WIDTH), 1).astype(F32)
    deltas = jnp.abs(HY_MIN_DECAY + ch * ((HY_MAX_DECAY - HY_MIN_DECAY) / (HY_WIDTH - 1)))
    g_ref[...] = jnp.where(m == L, 0.0, k * jnp.exp(-t * deltas))


def _hy_filter(L, w1, b1, w_inner, b_inner, freq, w_out, tr=512):
    n = 2 * L
    full = lambda *shape: pl.BlockSpec(shape, lambda i: (0,) * len(shape))
    col = lambda a: a.reshape(-1, 1)
    w1t = w1.T
    return pl.pallas_call(
        functools.partial(_hy_filter_body, L=L, tr=tr),
        grid=(n // tr,),
        in_specs=[full(HY_FILT, 1), full(HY_FILT, HY_BANDS), full(HY_FILT, HY_BANDS), full(HY_FILT, 1),
                  full(HY_INNER, HY_FILT, HY_FILT), full(HY_INNER, HY_FILT, 1), full(HY_FILT, 1),
                  pl.BlockSpec((HY_FILT, HY_WIDTH), lambda i: (0, i // (L // tr)))],
        out_specs=pl.BlockSpec((tr, HY_WIDTH), lambda i: (i, 0)),
        out_shape=jax.ShapeDtypeStruct((n, HY_WIDTH), F32),
        compiler_params=_params("parallel"),
        name="hy_filter",
    )(w1t[:, 0:1], w1t[:, 1:1 + HY_BANDS], w1t[:, 1 + HY_BANDS:], col(b1),
      jnp.swapaxes(w_inner, 1, 2), b_inner.reshape(HY_INNER, HY_FILT, 1), col(freq), w_out)


def _fft_dims(L):
    n = 2 * L
    n1 = n // FFT_N2
    return n, n1, n1 // 2 + SUBLANES, n1 // 2


@functools.lru_cache(maxsize=None)
def _fft_tables(L):
    n, n1, kh, h1 = _fft_dims(L)
    n2 = FFT_N2
    k1 = np.arange(kh, dtype=np.float64)[:, None]
    t1 = np.arange(n1, dtype=np.float64)[None, :]
    th = 2.0 * np.pi * k1 * t1 / n1
    fy = np.concatenate([np.cos(th), -np.sin(th)], axis=0)
    k2 = np.arange(n2, dtype=np.float64)[None, :, None]
    t2 = np.arange(n2, dtype=np.float64)[None, None, :]
    ph = 2.0 * np.pi * (k2 * t2 / n2 + k1[:, :, None] * t2 / n)
    mr, mi = np.cos(ph), -np.sin(ph)
    mx = np.concatenate([np.concatenate([mr, -mi], axis=2),
                         np.concatenate([mi, mr], axis=2)], axis=1)
    mxi = np.transpose(mx, (0, 2, 1))
    wk = np.zeros((kh,), np.float64)
    wk[:n1 // 2 + 1] = 2.0
    wk[0] = 1.0
    wk[n1 // 2] = 1.0
    thi = th[:, :h1].T
    fyi = np.concatenate([np.cos(thi) * wk[None, :], -np.sin(thi) * wk[None, :]], axis=1)
    return fy, mx, mxi, fyi


def _fft_unroll(kh):
    return max(u for u in range(1, FFT_MAX_UNROLL + 1) if kh % u == 0)


def _lane_cat(parts):
    return parts[0] if len(parts) == 1 else jnp.concatenate(parts, axis=1)


def _pitch_rows(load, dst_ref, nblk, nb):
    def body(t1, carry):
        s = pl.multiple_of(t1 * FFT_N2, SUBLANES)
        d = pl.multiple_of(t1 * FFT_PITCH, SUBLANES)
        for e in range(nb):
            dst_ref[e, pl.ds(d, FFT_N2), :] = load(e, s)
        return carry
    lax.fori_loop(0, nblk, body, 0, unroll=COPY_UNROLL)


def _fft_stage_y(src_ref, fy_ref, ar_ref, ai_ref, nblk, kh, nb):
    fy = fy_ref[...]
    for t2 in range(FFT_N2):
        x = _lane_cat([src_ref[e, pl.ds(t2, nblk, stride=FFT_PITCH), :] for e in range(nb)]).astype(BF16)
        a = jnp.dot(fy, x, preferred_element_type=F32)
        for e in range(nb):
            ar_ref[e, pl.ds(t2, kh, stride=FFT_PITCH), :] = a[:kh, e * LANES:(e + 1) * LANES]
            ai_ref[e, pl.ds(t2, kh, stride=FFT_PITCH), :] = a[kh:, e * LANES:(e + 1) * LANES]


def _hy_spectrum_body(g_ref, fy_ref, mx_ref, gr_ref, gi_ref, gs_ref, ar_ref, ai_ref, *, L):
    n, n1, kh, _ = _fft_dims(L)
    _pitch_rows(lambda e, s: g_ref[pl.ds(s, FFT_N2), :], gs_ref, n1, 1)
    _fft_stage_y(gs_ref, fy_ref, ar_ref, ai_ref, n1, kh, 1)
    unroll = _fft_unroll(kh)

    def body(it, carry):
        k1s = [it * unroll + u for u in range(unroll)]
        a = []
        for k1 in k1s:
            src = pl.multiple_of(k1 * FFT_PITCH, SUBLANES)
            a.append(jnp.concatenate([ar_ref[0, pl.ds(src, FFT_N2), :], ai_ref[0, pl.ds(src, FFT_N2), :]],
                                     axis=0).astype(BF16))
        z = [jnp.dot(mx_ref[k1], x, preferred_element_type=F32) * (1.0 / n) for k1, x in zip(k1s, a)]
        for k1, zz in zip(k1s, z):
            dst = pl.multiple_of(k1 * FFT_N2, SUBLANES)
            gr_ref[pl.ds(dst, FFT_N2), :] = zz[:FFT_N2]
            gi_ref[pl.ds(dst, FFT_N2), :] = zz[FFT_N2:]
        return carry
    lax.fori_loop(0, kh // unroll, body, 0)


def _hy_spectrum(g, L):
    n, n1, kh, _ = _fft_dims(L)
    fy, mx, _, _ = _fft_tables(L)
    out_spec = pl.BlockSpec((kh * FFT_N2, LANES), lambda c: (0, c))
    return pl.pallas_call(
        functools.partial(_hy_spectrum_body, L=L),
        grid=(HY_WIDTH // LANES,),
        in_specs=[pl.BlockSpec((n, LANES), lambda c: (0, c)),
                  pl.BlockSpec((2 * kh, n1), lambda c: (0, 0)),
                  pl.BlockSpec((kh, 2 * FFT_N2, 2 * FFT_N2), lambda c: (0, 0, 0))],
        out_specs=[out_spec, out_spec],
        out_shape=[jax.ShapeDtypeStruct((kh * FFT_N2, HY_WIDTH), F32)] * 2,
        scratch_shapes=[pltpu.VMEM((1, n1 * FFT_PITCH, LANES), F32),
                        pltpu.VMEM((1, kh * FFT_PITCH, LANES), F32),
                        pltpu.VMEM((1, kh * FFT_PITCH, LANES), F32)],
        compiler_params=_params("parallel"),
        name="hy_spectrum",
    )(g, jnp.asarray(fy, BF16), jnp.asarray(mx, BF16))


def _hy_conv_body(z_ref, x0_ref, gr_ref, gi_ref, fy_ref, mx_ref, mxi_ref, fyi_ref, skip_ref, o_ref,
                  zs_ref, ar_ref, ai_ref, *, L, nb):
    n, n1, kh, h1 = _fft_dims(L)
    seqs = range(nb)
    _pitch_rows(lambda e, s: z_ref[e, pl.ds(s, FFT_N2), :], zs_ref, h1, nb)
    _fft_stage_y(zs_ref, fy_ref, ar_ref, ai_ref, h1, kh, nb)

    unroll = _fft_unroll(kh)

    def spectral(it, carry):
        k1s = [it * unroll + u for u in range(unroll)]
        rows = [pl.multiple_of(k1 * FFT_PITCH, SUBLANES) for k1 in k1s]
        grows = [pl.multiple_of(k1 * FFT_N2, SUBLANES) for k1 in k1s]
        a = [_lane_cat([jnp.concatenate([ar_ref[e, pl.ds(r, FFT_N2), :], ai_ref[e, pl.ds(r, FFT_N2), :]], axis=0)
                        for e in seqs]).astype(BF16) for r in rows]
        z = [jnp.dot(mx_ref[k1], x, preferred_element_type=F32) for k1, x in zip(k1s, a)]
        y = []
        for zz, g in zip(z, grows):
            zr, zi = zz[:FFT_N2], zz[FFT_N2:]
            gr = _lane_cat([gr_ref[pl.ds(g, FFT_N2), :]] * nb)
            gi = _lane_cat([gi_ref[pl.ds(g, FFT_N2), :]] * nb)
            y.append(jnp.concatenate([zr * gr - zi * gi, zr * gi + zi * gr], axis=0).astype(BF16))
        b = [jnp.dot(mxi_ref[k1], x, preferred_element_type=F32) for k1, x in zip(k1s, y)]
        for r, bb in zip(rows, b):
            for e in seqs:
                ar_ref[e, pl.ds(r, FFT_N2), :] = bb[:FFT_N2, e * LANES:(e + 1) * LANES]
                ai_ref[e, pl.ds(r, FFT_N2), :] = bb[FFT_N2:, e * LANES:(e + 1) * LANES]
        return carry
    lax.fori_loop(0, kh // unroll, spectral, 0)

    fyi = fyi_ref[...]
    for t2 in range(FFT_N2):
        b = _lane_cat([jnp.concatenate([ar_ref[e, pl.ds(t2, kh, stride=FFT_PITCH), :],
                                        ai_ref[e, pl.ds(t2, kh, stride=FFT_PITCH), :]], axis=0) for e in seqs])
        y = jnp.dot(fyi, b.astype(BF16), preferred_element_type=F32)
        for e in seqs:
            zs_ref[e, pl.ds(t2, h1, stride=FFT_PITCH), :] = y[:, e * LANES:(e + 1) * LANES]

    skip = skip_ref[...]

    def finish(t1, carry):
        s = pl.multiple_of(t1 * FFT_N2, SUBLANES)
        d = pl.multiple_of(t1 * FFT_PITCH, SUBLANES)
        for e in seqs:
            y = zs_ref[e, pl.ds(d, FFT_N2), :] + z_ref[e, pl.ds(s, FFT_N2), :] * skip
            o_ref[e, pl.ds(s, FFT_N2), :] = (x0_ref[e, pl.ds(s, FFT_N2), :] * y).astype(o_ref.dtype)
        return carry
    lax.fori_loop(0, h1, finish, 0, unroll=COPY_UNROLL)


def _hy_conv(z, x0c, gr, gi, skip, L, nb=2):
    bsz = z.shape[0]
    n, n1, kh, h1 = _fft_dims(L)
    fy, mx, mxi, fyi = _fft_tables(L)
    seq = pl.BlockSpec((nb, L, LANES), lambda c, b: (b, 0, c))
    fixed = lambda shape, imap: pl.BlockSpec(shape, imap, pipeline_mode=pl.Buffered(1))
    spec = fixed((kh * FFT_N2, LANES), lambda c, b: (0, c))
    mat = fixed((kh, 2 * FFT_N2, 2 * FFT_N2), lambda c, b: (0, 0, 0))
    return pl.pallas_call(
        functools.partial(_hy_conv_body, L=L, nb=nb),
        grid=(HY_WIDTH // LANES, bsz // nb),
        in_specs=[seq, seq, spec, spec,
                  pl.BlockSpec((2 * kh, h1), lambda c, b: (0, 0)),
                  mat, mat,
                  pl.BlockSpec((h1, 2 * kh), lambda c, b: (0, 0)),
                  pl.BlockSpec((1, LANES), lambda c, b: (0, c))],
        out_specs=seq,
        out_shape=jax.ShapeDtypeStruct((bsz, L, HY_WIDTH), BF16),
        scratch_shapes=[pltpu.VMEM((nb, h1 * FFT_PITCH, LANES), F32),
                        pltpu.VMEM((nb, kh * FFT_PITCH, LANES), F32),
                        pltpu.VMEM((nb, kh * FFT_PITCH, LANES), F32)],
        compiler_params=_params("parallel", "parallel"),
        name="hy_conv",
    )(z, x0c, gr, gi, jnp.asarray(fy[:, :h1], BF16), jnp.asarray(mx, BF16), jnp.asarray(mxi, BF16),
      jnp.asarray(fyi, BF16), skip.reshape(1, -1))


def _rms(x, g):
    return x * lax.rsqrt(jnp.mean(x * x, axis=-1, keepdims=True) + RMS_EPS) * g


def _mla_proj_body(p_ref, qn_ref, wq_ref, kvn_ref, wk_ref, wvt_ref,
                   qc_ref, qs_ref, kt_ref, q_ref, k_ref, vt_ref):
    p = p_ref[...]
    cq = _rms(p[:, :MLA_Q_RANK], qn_ref[...]).astype(BF16)
    ckv = _rms(p[:, MLA_Q_RANK:MLA_Q_RANK + MLA_KV_RANK], kvn_ref[...]).astype(BF16)
    q = jnp.dot(cq, wq_ref[...], preferred_element_type=F32)
    k = jnp.dot(ckv, wk_ref[...], preferred_element_type=F32)
    kr = p[:, MLA_Q_RANK + MLA_KV_RANK:] * kt_ref[...]
    kr = kr + pltpu.roll(kr, LANES - MLA_ROPE, axis=1)
    lane = lax.broadcasted_iota(jnp.int32, (1, HEAD_PAD), 1)
    in_rope = (lane >= MLA_NOPE) & (lane < MLA_NOPE + MLA_ROPE)
    kr = jnp.where(in_rope, pltpu.roll(kr, MLA_NOPE, axis=1), 0.0)
    qc = qc_ref[...]
    qs = qs_ref[...]
    for h in range(MLA_HEADS):
        sl = slice(h * HEAD_PAD, (h + 1) * HEAD_PAD)
        x = q[:, sl]
        q_ref[:, sl] = (x * qc + pltpu.roll(x, HEAD_PAD - MLA_ROPE, axis=1) * qs).astype(BF16)
        k_ref[:, sl] = (k[:, sl] + kr).astype(BF16)
    vt = lax.dot_general(wvt_ref[...], ckv, (((1,), (1,)), ((), ())), preferred_element_type=F32)
    row = lax.broadcasted_iota(jnp.int32, (MLA_HEADS * VT_ROWS, 1), 0) % VT_ROWS
    vt_ref[0] = (vt + jnp.where(row == MLA_V, 1.0, 0.0)).astype(BF16)


def _mla_proj(p, L, q_norm, wq, kv_norm, wk, wvt, qc, qs, kt, tm=512):
    t = p.shape[0]
    hw = MLA_HEADS * HEAD_PAD
    nper = L // tm
    fix = lambda i: (0, 0)
    pos = lambda i: (i % nper, 0)
    row = lambda i: (i, 0)
    wide = 6 * LANES
    return pl.pallas_call(
        _mla_proj_body,
        grid=(t // tm,),
        in_specs=[pl.BlockSpec((tm, wide), row),
                  pl.BlockSpec((1, MLA_Q_RANK), fix),
                  pl.BlockSpec((MLA_Q_RANK, hw), fix),
                  pl.BlockSpec((1, MLA_KV_RANK), fix),
                  pl.BlockSpec((MLA_KV_RANK, hw), fix),
                  pl.BlockSpec((MLA_HEADS * VT_ROWS, MLA_KV_RANK), fix),
                  pl.BlockSpec((tm, HEAD_PAD), pos), pl.BlockSpec((tm, HEAD_PAD), pos),
                  pl.BlockSpec((tm, LANES), pos)],
        out_specs=[pl.BlockSpec((tm, hw), row), pl.BlockSpec((tm, hw), row),
                   pl.BlockSpec((1, MLA_HEADS * VT_ROWS, tm), lambda i: (i // nper, 0, i % nper))],
        out_shape=[jax.ShapeDtypeStruct((t, hw), BF16), jax.ShapeDtypeStruct((t, hw), BF16),
                   jax.ShapeDtypeStruct((t // L, MLA_HEADS * VT_ROWS, L), BF16)],
        compiler_params=_params("parallel"),
        name="mla_proj",
    )(p, q_norm, wq, kv_norm, wk, wvt, qc, qs, kt)


def _attn_body(q_ref, k_ref, vt_ref, o_ref, *, kc):
    L = k_ref.shape[1]
    nchunk = L // kc
    heads = range(2)
    qs = [q_ref[0, :, h * HEAD_PAD:(h + 1) * HEAD_PAD] for h in heads]

    def scores(h, c):
        return lax.dot_general(k_ref[0, c * kc:(c + 1) * kc, h * HEAD_PAD:(h + 1) * HEAD_PAD], qs[h],
                               (((1,), (1,)), ((), ())), preferred_element_type=F32)

    m = [None, None]
    acc = [None, None]
    s_next = [scores(h, 0) for h in heads]
    for c in range(nchunk):
        s_cur = s_next
        if c + 1 < nchunk:
            s_next = [scores(h, c + 1) for h in heads]
        for h in heads:
            s = s_cur[h]
            cmax = jnp.max(s, axis=0, keepdims=True)
            m_new = cmax if c == 0 else jnp.maximum(m[h], cmax)
            e = jnp.exp2(s - m_new).astype(BF16)
            pv = jnp.dot(vt_ref[0, h * VT_ROWS:(h + 1) * VT_ROWS, c * kc:(c + 1) * kc], e,
                         preferred_element_type=F32)
            acc[h] = pv if c == 0 else acc[h] * jnp.exp2(m[h] - m_new) + pv
            m[h] = m_new
    out_t = jnp.concatenate([acc[h][:MLA_V] / acc[h][MLA_V:MLA_V + 1] for h in heads], axis=0)
    o_ref[0] = out_t.T.astype(o_ref.dtype)


def _attention(q, k, vt, tq=1024, kc=256):
    bsz, L, _ = q.shape
    return pl.pallas_call(
        functools.partial(_attn_body, kc=kc),
        grid=(bsz, MLA_HEADS // 2, L // tq),
        in_specs=[pl.BlockSpec((1, tq, 2 * HEAD_PAD), lambda b, g, i: (b, i, g)),
                  pl.BlockSpec((1, L, 2 * HEAD_PAD), lambda b, g, i: (b, 0, g)),
                  pl.BlockSpec((1, 2 * VT_ROWS, L), lambda b, g, i: (b, g, 0))],
        out_specs=pl.BlockSpec((1, tq, 2 * MLA_V), lambda b, g, i: (b, i, g)),
        out_shape=jax.ShapeDtypeStruct((bsz, L, MLA_HEADS * MLA_V), BF16),
        compiler_params=_params("parallel", "parallel", "parallel"),
        name="attention",
    )(q, k, vt)


def _odd_mix_body(um_ref, ul_ref, ur_ref, dw_ref, db_ref, lg_ref, lb_ref, pw_ref, ps_ref,
                  cf_ref, pool_ref, h_ref, u_ref, conv_ref, hp_ref, sa_ref, sb_ref, *, tl, L, rows):
    i = pl.program_id(1)
    last = pl.num_programs(1) - 1

    def glu(u):
        return u[:, :CF_WIDTH] * jax.nn.sigmoid(u[:, CF_WIDTH:2 * CF_WIDTH])

    ul = ul_ref[0]
    um = um_ref[0]
    ur = ur_ref[0]
    h_ref[0:HALO, :] = jnp.where(i == 0, 0.0, glu(ul))
    h_ref[HALO:HALO + tl, :] = glu(um)
    h_ref[HALO + tl:, :] = jnp.where(i == last, 0.0, glu(ur))
    u_ref[0:HALO, :] = jnp.where(i == 0, 0.0, ul[:, 2 * CF_WIDTH:])
    u_ref[HALO:HALO + tl, :] = um[:, 2 * CF_WIDTH:]
    u_ref[HALO + tl:, :] = jnp.where(i == last, 0.0, ur[:, 2 * CF_WIDTH:])

    span = tl + 2 * HALO - SUBLANES
    for j in range(SUBLANES):
        hp_ref[j, 0:span, :] = h_ref[j:j + span, :]

    off = HALO - CF_KERNEL // 2
    for r in range(0, tl, rows):
        for c in range(0, CF_WIDTH, LANES):
            acc = jnp.zeros((rows, LANES), F32) + db_ref[:, c:c + LANES]
            for k in range(CF_KERNEL):
                j = (off + k) % SUBLANES
                base = r + off + k - j
                acc = acc + dw_ref[k:k + 1, c:c + LANES] * hp_ref[j, base:base + rows, c:c + LANES]
            conv_ref[r:r + rows, c:c + LANES] = acc
    hn = _layer_norm(conv_ref[...], lg_ref[...], lb_ref[...])
    cf_ref[0] = (hn * jax.nn.sigmoid(hn)).astype(cf_ref.dtype)

    pos = i * tl + lax.broadcasted_iota(jnp.int32, (tl, 1), 0)
    ext = tl + 2 * HALO
    first, stop = SUBLANES, ext - SUBLANES
    for buf in (sa_ref, sb_ref):
        buf[stop:ext, :] = jnp.zeros((SUBLANES, POOL_GROUP), F32)

    def window_sum(w, cols):
        lo = w // 2
        src = lambda a, b: u_ref[a:b, cols]
        width, bufs = 1, [sa_ref, sb_ref]
        while 2 * width < w:
            dst = bufs[0]
            dst[first:stop, :] = src(first, stop) + src(first + width, stop + width)
            src = functools.partial(lambda d, a, b: d[a:b, :], dst)
            bufs.reverse()
            width *= 2
        a0 = HALO - lo
        return src(a0, a0 + tl) + src(a0 + width, a0 + width + tl)

    for gi, w in enumerate(POOL_WINDOWS):
        lo = w // 2
        hi = w - 1 - lo
        c0 = gi * POOL_GROUP
        tot = window_sum(w, slice(c0, c0 + POOL_GROUP))
        cnt = (jnp.minimum(pos + hi + 1, L) - jnp.maximum(pos - lo, 0)).astype(F32)
        dlt = tot / cnt - u_ref[HALO:HALO + tl, c0:c0 + POOL_GROUP]
        pool_ref[0, :, c0:c0 + POOL_GROUP] = (
            jnp.dot(dlt.astype(BF16), pw_ref[gi], preferred_element_type=F32)
            * ps_ref[:, c0:c0 + POOL_GROUP]).astype(pool_ref.dtype)


def _odd_mix(u, dw_w, dw_b, ln_g, ln_b, pool_w, pool_scale, tl=512, rows=128):
    bsz, L, w = u.shape
    main, left, right = _halo_specs(tl, L)
    fix2 = lambda b, i: (0, 0)
    out_spec = pl.BlockSpec((1, tl, CF_WIDTH), main)
    return pl.pallas_call(
        functools.partial(_odd_mix_body, tl=tl, L=L, rows=rows),
        grid=(bsz, L // tl),
        in_specs=[pl.BlockSpec((1, tl, w), main),
                  pl.BlockSpec((1, HALO, w), left),
                  pl.BlockSpec((1, HALO, w), right),
                  pl.BlockSpec((CF_KERNEL, CF_WIDTH), fix2),
                  pl.BlockSpec((1, CF_WIDTH), fix2), pl.BlockSpec((1, CF_WIDTH), fix2),
                  pl.BlockSpec((1, CF_WIDTH), fix2),
                  pl.BlockSpec((len(POOL_WINDOWS), POOL_GROUP, POOL_GROUP), lambda b, i: (0, 0, 0)),
                  pl.BlockSpec((1, POOL_WIDTH), fix2)],
        out_specs=[out_spec, out_spec],
        out_shape=[jax.ShapeDtypeStruct((bsz, L, CF_WIDTH), BF16)] * 2,
        scratch_shapes=[pltpu.VMEM((tl + 2 * HALO, CF_WIDTH), F32),
                        pltpu.VMEM((tl + 2 * HALO, POOL_WIDTH), F32),
                        pltpu.VMEM((tl, CF_WIDTH), F32),
                        pltpu.VMEM((SUBLANES, tl + 2 * HALO, CF_WIDTH), F32),
                        pltpu.VMEM((tl + 2 * HALO, POOL_GROUP), F32),
                        pltpu.VMEM((tl + 2 * HALO, POOL_GROUP), F32)],
        compiler_params=_params("parallel", "parallel"),
        name="odd_mix",
    )(u, u, u, dw_w, dw_b, ln_g, ln_b, pool_w, pool_scale)


def _rope_tables(L):
    inv = 1.0 / (ROPE_THETA ** (jnp.arange(0, MLA_ROPE, 2, dtype=F32) / MLA_ROPE))
    ang = jnp.arange(L, dtype=F32)[:, None] * inv[None, :]
    cos, sin = jnp.cos(ang), jnp.sin(ang)
    scale = (MLA_NOPE + MLA_ROPE) ** -0.5 * math.log2(math.e)
    ones = jnp.ones((L, MLA_NOPE), F32)
    zeros = jnp.zeros((L, MLA_NOPE), F32)
    pad = jnp.zeros((L, HEAD_PAD - MLA_NOPE - MLA_ROPE), F32)
    qc = scale * jnp.concatenate([ones, cos, cos, pad], axis=1)
    qs = scale * jnp.concatenate([zeros, sin, sin, pad], axis=1)
    kt = jnp.concatenate([cos, cos, sin, sin, jnp.zeros((L, LANES - 2 * MLA_ROPE), F32)], axis=1)
    return qc, qs, kt


def _rot_half_cols(w):
    half = w.shape[-1] // 2
    return jnp.concatenate([-w[..., half:], w[..., :half]], axis=-1)


def _prep_even(ev_w_in, mla_w_uq, mla_w_ukv):
    k = ev_w_in.shape[0]
    base = 3 * HY_WIDTH + MLA_Q_RANK + MLA_KV_RANK
    kr = ev_w_in[:, base:base + MLA_ROPE]
    w_in = jnp.concatenate(
        [ev_w_in[:, :base], kr, _rot_half_cols(kr),
         jnp.zeros((k, LANES - 2 * MLA_ROPE), F32)], axis=1).astype(BF16)
    uq = mla_w_uq.reshape(MLA_Q_RANK, MLA_HEADS, MLA_NOPE + MLA_ROPE)
    wq = jnp.concatenate([uq, _rot_half_cols(uq[:, :, MLA_NOPE:])], axis=2).reshape(MLA_Q_RANK, -1).astype(BF16)
    ukv = mla_w_ukv.reshape(MLA_KV_RANK, MLA_HEADS, MLA_NOPE + MLA_V)
    zk = jnp.zeros((MLA_KV_RANK, MLA_HEADS, HEAD_PAD - MLA_NOPE), F32)
    wk = jnp.concatenate([ukv[:, :, :MLA_NOPE], zk], axis=2).reshape(MLA_KV_RANK, -1).astype(BF16)
    zv = jnp.zeros((MLA_KV_RANK, MLA_HEADS, VT_ROWS - MLA_V), F32)
    wvt = jnp.concatenate([ukv[:, :, MLA_NOPE:], zv], axis=2).reshape(MLA_KV_RANK, -1).T.astype(BF16)
    return w_in, wq, wk, wvt


def _trunk(x, w, filt):
    bsz, L, d = x.shape
    t = bsz * L
    xf = x.reshape(t, d)
    qc, qs, kt = _rope_tables(L)
    zero_b = jnp.zeros((1, d), F32)
    for layer in range(DEPTH):
        i = layer // 2
        if layer % 2 == 0:
            w_in, wq, wk, wvt = w["even"][i]
            z, x0c, p_mla = _even_in(xf.reshape(bsz, L, d), w_in, w["hy_conv_w"][i],
                                     w["hy_conv_b"][i].reshape(1, -1))
            gr, gi = filt[(L, i)]
            m1 = _hy_conv(z, x0c, gr, gi, w["hy_skip"][i], L).reshape(t, HY_WIDTH)
            q, k, vt = _mla_proj(p_mla.reshape(t, -1), L, w["mla_q_norm"][i].reshape(1, -1), wq,
                                 w["mla_kv_norm"][i].reshape(1, -1), wk, wvt, qc, qs, kt)
            hw = MLA_HEADS * HEAD_PAD
            m2 = _attention(q.reshape(bsz, L, hw), k.reshape(bsz, L, hw), vt).reshape(t, MLA_HEADS * MLA_V)
            w_out = w["ev_w_out"][i]
            b_out = zero_b
        else:
            cf, pool = _odd_mix(u.reshape(bsz, L, -1), w["cf_dw_w"][i], w["cf_dw_b"][i].reshape(1, -1),
                                w["cf_ln_g"][i].reshape(1, -1), w["cf_ln_b"][i].reshape(1, -1),
                                w["pool_w"][i], w["pool_scale"][i].reshape(1, -1))
            m1 = cf.reshape(t, CF_WIDTH)
            m2 = pool.reshape(t, POOL_WIDTH)
            w_out = w["od_w_out"][i]
            b_out = w["od_b_out"][i].reshape(1, -1)
        half = w_out.shape[0] // 2
        tail_args = (m1, m2, xf, w_out[:half], w_out[half:], b_out,
                     w["ln1_g"][layer].reshape(1, -1), w["ln1_b"][layer].reshape(1, -1),
                     w["mlp_w1"][layer], w["mlp_w2"][layer],
                     w["ln2_g"][layer].reshape(1, -1), w["ln2_b"][layer].reshape(1, -1))
        if layer % 2 == 0 and layer + 1 < DEPTH:
            xf, u = _layer_tail(*tail_args, w["od_w_in"][i], w["od_b_in"][i].reshape(1, -1))
        else:
            xf = _layer_tail(*tail_args)
    return xf.reshape(bsz, L, d)


def kernel(x_prompt, x_sample, ev_w_in, hy_conv_w, hy_conv_b, hy_filt_w1, hy_filt_b1, hy_filt_w_inner, hy_filt_b_inner, hy_filt_freq, hy_filt_w_out, hy_skip, mla_q_norm, mla_w_uq, mla_kv_norm, mla_w_ukv, ev_w_out, od_w_in, od_b_in, cf_dw_w, cf_dw_b, cf_ln_g, cf_ln_b, pool_w, pool_scale, od_w_out, od_b_out, ln1_g, ln1_b, mlp_w1, mlp_w2, ln2_g, ln2_b):
    n_even = ev_w_in.shape[0]
    w = {
        "even": [_prep_even(ev_w_in[i], mla_w_uq[i], mla_w_ukv[i]) for i in range(n_even)],
        "hy_conv_w": hy_conv_w, "hy_conv_b": hy_conv_b, "hy_skip": hy_skip,
        "mla_q_norm": mla_q_norm, "mla_kv_norm": mla_kv_norm,
        "ev_w_out": ev_w_out.astype(BF16),
        "od_w_in": od_w_in.astype(BF16), "od_b_in": od_b_in,
        "cf_dw_w": cf_dw_w, "cf_dw_b": cf_dw_b, "cf_ln_g": cf_ln_g, "cf_ln_b": cf_ln_b,
        "pool_w": pool_w.astype(BF16), "pool_scale": pool_scale,
        "od_w_out": od_w_out.astype(BF16), "od_b_out": od_b_out,
        "ln1_g": ln1_g, "ln1_b": ln1_b, "ln2_g": ln2_g, "ln2_b": ln2_b,
        "mlp_w1": mlp_w1.astype(BF16), "mlp_w2": mlp_w2.astype(BF16),
    }
    filt = {}
    for L in sorted({x_prompt.shape[1], x_sample.shape[1]}):
        for i in range(n_even):
            g = _hy_filter(L, hy_filt_w1[i], hy_filt_b1[i], hy_filt_w_inner[i], hy_filt_b_inner[i],
                           hy_filt_freq[i], hy_filt_w_out[i])
            filt[(L, i)] = _hy_spectrum(g, L)
    return (_trunk(x_prompt, w, filt), _trunk(x_sample, w, filt))
```
